```python
import jax, jax.numpy as jnp
from jax import lax
import numpy as np

D_MODEL = 1024
BATCH = 16
SEQ = 4096
DEPTH = 4
DEC_BATCH = 1
DEC_SEQ = 16384
PAST_LEN = 128

N_MIXERS = 2
N_ATT_LAYERS = (DEPTH + 1) // 2
N_CONV_LAYERS = DEPTH // 2
N_HEADS = 16
N_KV_HEADS = 4
HEAD_DIM = D_MODEL // N_HEADS
GROUP = N_HEADS // N_KV_HEADS
WINDOW = 128
BLOCK = 128
SPAN = BLOCK + 2 * WINDOW
CONV_WIDTH = 31
CONV_PAD = CONV_WIDTH // 2
D_FF = -(-8 * D_MODEL // (3 * 256)) * 256
EPS = 1e-6
NEG_INF = -1e30

kernel_name = "hybrid_swa_conformer_encoder"


def rmsnorm(x, g):
    xf = x.astype(jnp.float32)
    y = xf * lax.rsqrt(jnp.mean(xf * xf, axis=-1, keepdims=True) + EPS)
    return (y * g.astype(jnp.float32)).astype(x.dtype)


def layernorm(x, g, b):
    xf = x.astype(jnp.float32)
    mu = jnp.mean(xf, axis=-1, keepdims=True)
    var = jnp.mean(jnp.square(xf - mu), axis=-1, keepdims=True)
    y = (xf - mu) * lax.rsqrt(var + EPS)
    return (y * g.astype(jnp.float32) + b.astype(jnp.float32)).astype(x.dtype)


def alibi_slopes():
    h = np.arange(1, N_HEADS + 1, dtype=np.float32)
    return jnp.asarray(np.power(2.0, -8.0 * h / N_HEADS).astype(np.float32)).reshape(N_KV_HEADS, GROUP)


def windowed_gqa(x, w_qkv, w_o, sink):
    B, S, _ = x.shape
    qkv = x @ w_qkv
    nq = N_HEADS * HEAD_DIM
    nk = N_KV_HEADS * HEAD_DIM
    q = qkv[..., :nq].reshape(B, S, N_KV_HEADS, GROUP, HEAD_DIM) * (HEAD_DIM ** -0.5)
    k = qkv[..., nq:nq + nk].reshape(B, S, N_KV_HEADS, HEAD_DIM)
    v = qkv[..., nq + nk:].reshape(B, S, N_KV_HEADS, HEAD_DIM)
    pad = ((0, 0), (WINDOW, WINDOW), (0, 0), (0, 0))
    kp = jnp.pad(k, pad)
    vp = jnp.pad(v, pad)
    slopes = alibi_slopes()
    sink_f = sink.astype(jnp.float32).reshape(N_KV_HEADS, GROUP)[None, :, :, None]
    dist = jnp.abs(jnp.arange(BLOCK)[:, None] - jnp.arange(SPAN)[None, :] + WINDOW).astype(jnp.float32)
    band = dist <= WINDOW
    alibi = -slopes[:, :, None, None] * dist

    def block(j):
        start = j * BLOCK
        qb = lax.dynamic_slice_in_dim(q, start, BLOCK, axis=1)
        kb = lax.dynamic_slice_in_dim(kp, start, SPAN, axis=1)
        vb = lax.dynamic_slice_in_dim(vp, start, SPAN, axis=1)
        s_pos = start - WINDOW + jnp.arange(SPAN)
        valid = band & ((s_pos >= 0) & (s_pos < S))[None, :]
        sc = jnp.einsum('bqkgd,bskd->bkgqs', qb, kb).astype(jnp.float32) + alibi
        sc = jnp.where(valid, sc, NEG_INF)
        m = jnp.maximum(jnp.max(sc, axis=-1), sink_f)
        p = jnp.exp(sc - m[..., None])
        denom = jnp.sum(p, axis=-1) + jnp.exp(sink_f - m)
        p = (p / denom[..., None]).astype(vb.dtype)
        return jnp.einsum('bkgqs,bskd->bqkgd', p, vb)

    out = lax.map(block, jnp.arange(S // BLOCK))
    out = jnp.moveaxis(out, 0, 1).reshape(B, S, N_HEADS * HEAD_DIM)
    return out @ w_o


def conformer_conv(x, w_in, b_in, w_dw, b_dw, ln_g, ln_b, w_out, b_out):
    h = x @ w_in + b_in
    a, gate = jnp.split(h, 2, axis=-1)
    h = a * jax.nn.sigmoid(gate)
    h = lax.conv_general_dilated(h, w_dw[:, None, :].astype(h.dtype), window_strides=(1,),
                                 padding=[(CONV_PAD, CONV_PAD)],
                                 dimension_numbers=('NWC', 'WIO', 'NWC'),
                                 feature_group_count=D_MODEL) + b_dw
    h = jax.nn.silu(layernorm(h, ln_g, ln_b))
    return h @ w_out + b_out


def swiglu_ffn(x, w_gu, w_down):
    g, u = jnp.split(x @ w_gu, 2, axis=-1)
    return (jax.nn.silu(g) * u) @ w_down


def trunk(x, g_mix_pre, g_mix_post, g_ffn_pre, g_ffn_post, w_qkv, w_o, attn_sink,
          conv_w_in, conv_b_in, conv_w_dw, conv_b_dw, conv_ln_g, conv_ln_b, conv_w_out, conv_b_out,
          ffn_w_gu, ffn_w_down):
    for i in range(DEPTH):
        h = rmsnorm(x, g_mix_pre[i])
        li = i // N_MIXERS
        if i % N_MIXERS == 0:
            h = windowed_gqa(h, w_qkv[li], w_o[li], attn_sink[li])
        else:
            h = conformer_conv(h, conv_w_in[li], conv_b_in[li], conv_w_dw[li], conv_b_dw[li],
                               conv_ln_g[li], conv_ln_b[li], conv_w_out[li], conv_b_out[li])
        x = x + rmsnorm(h, g_mix_post[i])
        h = swiglu_ffn(rmsnorm(x, g_ffn_pre[i]), ffn_w_gu[i], ffn_w_down[i])
        x = x + rmsnorm(h, g_ffn_post[i])
    return x


def setup_inputs(seed: int = 0) -> dict:
    key = jax.random.key(seed)
    ks = jax.random.split(key, 24)
    f32 = jnp.float32
    D = D_MODEL
    qkv_out = (N_HEADS + 2 * N_KV_HEADS) * HEAD_DIM

    def nrm(k, shape, scale):
        return jax.random.normal(k, shape, f32) * scale

    def gain(k, shape):
        return 1.0 + 0.02 * jax.random.normal(k, shape, f32)

    return {
        "x_prompt": nrm(ks[0], (BATCH, SEQ, D), 1.0),
        "x_sample": nrm(ks[1], (DEC_BATCH, DEC_SEQ, D), 1.0),
        "g_mix_pre": gain(ks[2], (DEPTH, D)),
        "g_mix_post": gain(ks[3], (DEPTH, D)),
        "g_ffn_pre": gain(ks[4], (DEPTH, D)),
        "g_ffn_post": gain(ks[5], (DEPTH, D)),
        "w_qkv": nrm(ks[6], (N_ATT_LAYERS, D, qkv_out), D ** -0.5),
        "w_o": nrm(ks[7], (N_ATT_LAYERS, N_HEADS * HEAD_DIM, D), (N_HEADS * HEAD_DIM) ** -0.5),
        "attn_sink": nrm(ks[8], (N_ATT_LAYERS, N_HEADS), 0.5),
        "conv_w_in": nrm(ks[9], (N_CONV_LAYERS, D, 2 * D), D ** -0.5),
        "conv_b_in": nrm(ks[10], (N_CONV_LAYERS, 2 * D), 0.02),
        "conv_w_dw": nrm(ks[11], (N_CONV_LAYERS, CONV_WIDTH, D), CONV_WIDTH ** -0.5),
        "conv_b_dw": nrm(ks[12], (N_CONV_LAYERS, D), 0.02),
        "conv_ln_g": gain(ks[13], (N_CONV_LAYERS, D)),
        "conv_ln_b": nrm(ks[14], (N_CONV_LAYERS, D), 0.02),
        "conv_w_out": nrm(ks[15], (N_CONV_LAYERS, D, D), D ** -0.5),
        "conv_b_out": nrm(ks[16], (N_CONV_LAYERS, D), 0.02),
        "ffn_w_gu": nrm(ks[17], (DEPTH, D, 2 * D_FF), D ** -0.5),
        "ffn_w_down": nrm(ks[18], (DEPTH, D_FF, D), D_FF ** -0.5),
    }


def reference(x_prompt, x_sample, g_mix_pre, g_mix_post, g_ffn_pre, g_ffn_post, w_qkv, w_o, attn_sink,
              conv_w_in, conv_b_in, conv_w_dw, conv_b_dw, conv_ln_g, conv_ln_b, conv_w_out, conv_b_out,
              ffn_w_gu, ffn_w_down):
    y_prompt = trunk(x_prompt, g_mix_pre, g_mix_post, g_ffn_pre, g_ffn_post, w_qkv, w_o, attn_sink,
                     conv_w_in, conv_b_in, conv_w_dw, conv_b_dw, conv_ln_g, conv_ln_b, conv_w_out, conv_b_out,
                     ffn_w_gu, ffn_w_down)
    y_sample = trunk(x_sample, g_mix_pre, g_mix_post, g_ffn_pre, g_ffn_post, w_qkv, w_o, attn_sink,
                     conv_w_in, conv_b_in, conv_w_dw, conv_b_dw, conv_ln_g, conv_ln_b, conv_w_out, conv_b_out,
                     ffn_w_gu, ffn_w_down)
    return (y_prompt, y_sample)
```

```python
import functools

import jax
import jax.numpy as jnp
import numpy as np
from jax import lax
from jax.experimental import pallas as pl
from jax.experimental.pallas import tpu as pltpu

D_MODEL = 1024
DEPTH = 4
N_MIXERS = 2
N_HEADS = 16
N_KV_HEADS = 4
HEAD_DIM = D_MODEL // N_HEADS
GROUP = N_HEADS // N_KV_HEADS
WINDOW = 128
BLOCK = 128
SPAN = BLOCK + 2 * WINDOW
CONV_WIDTH = 31
CONV_PAD = CONV_WIDTH // 2
D_FF = -(-8 * D_MODEL // (3 * 256)) * 256
EPS = 1e-6
NEG_INF = -1e30

Q_COLS = N_HEADS * HEAD_DIM
KV_COLS = 2 * N_KV_HEADS * HEAD_DIM
CONV_HALO = 16

V7X_VMEM_BYTES = 64 * 1024 * 1024
VMEM_LIMIT_BYTES = 60000 * 1024
assert VMEM_LIMIT_BYTES < V7X_VMEM_BYTES

ROW_TILE = 512
ATTN_TILE = 512
FF_CHUNKS = (768, 768, 768, 512)
assert sum(FF_CHUNKS) == D_FF

_BF16 = jnp.bfloat16
_F32 = jnp.float32

_ALIBI_SLOPES = tuple(
    float(s) for s in np.power(2.0, -8.0 * np.arange(1, N_HEADS + 1, dtype=np.float32) / N_HEADS).astype(np.float32)
)


def _params(*semantics):
    return pltpu.CompilerParams(dimension_semantics=semantics, vmem_limit_bytes=VMEM_LIMIT_BYTES)


def _resident(shape):
    zeros = (0,) * len(shape)
    return pl.BlockSpec(shape, lambda *_: zeros, pipeline_mode=pl.Buffered(1))


def _rmsnorm(x, g):
    return x * lax.rsqrt(jnp.mean(x * x, axis=-1, keepdims=True) + EPS) * g


def _sigmoid(x):
    return 1.0 / (1.0 + jnp.exp(-x))


def _dot(a, b):
    return jnp.dot(a, b, preferred_element_type=_F32)


def _qkv_kernel(x_ref, g_ref, w_ref, q_ref, kv_ref):
    h = _rmsnorm(x_ref[...], g_ref[...]).astype(_BF16)
    qkv = _dot(h, w_ref[...])
    q_ref[...] = (qkv[:, :Q_COLS] * (HEAD_DIM ** -0.5)).astype(_BF16)
    kv_ref[...] = qkv[:, Q_COLS:].astype(_BF16)


def _qkv_proj(x2d, g, w):
    rows = x2d.shape[0]
    return pl.pallas_call(
        _qkv_kernel,
        grid=(rows // ROW_TILE,),
        in_specs=[
            pl.BlockSpec((ROW_TILE, D_MODEL), lambda i: (i, 0)),
            _resident((1, D_MODEL)),
            _resident((D_MODEL, Q_COLS + KV_COLS)),
        ],
        out_specs=[
            pl.BlockSpec((ROW_TILE, Q_COLS), lambda i: (i, 0)),
            pl.BlockSpec((ROW_TILE, KV_COLS), lambda i: (i, 0)),
        ],
        out_shape=[
            jax.ShapeDtypeStruct((rows, Q_COLS), _BF16),
            jax.ShapeDtypeStruct((rows, KV_COLS), _BF16),
        ],
        compiler_params=_params("parallel"),
        name="qkv_proj",
    )(x2d, g, w)


def _attn_kernel(seq_len, sink_ref, q_ref, kvp_ref, kvm_ref, kvn_ref, o_ref, kv_buf):
    i = pl.program_id(1)
    kv_buf[0:WINDOW, :] = kvp_ref[...]
    kv_buf[WINDOW:WINDOW + ATTN_TILE, :] = kvm_ref[...]
    kv_buf[WINDOW + ATTN_TILE:, :] = kvn_ref[...]

    row = lax.broadcasted_iota(jnp.int32, (BLOCK, SPAN), 0)
    col = lax.broadcasted_iota(jnp.int32, (BLOCK, SPAN), 1)
    dist_i = jnp.abs(row - col + WINDOW)
    band = dist_i <= WINDOW
    dist = dist_i.astype(_F32)
    k_v = N_KV_HEADS * HEAD_DIM

    for jb in range(ATTN_TILE // BLOCK):
        r0 = jb * BLOCK
        s_pos = i * ATTN_TILE + (r0 - WINDOW) + col
        valid = band & (s_pos >= 0) & (s_pos < seq_len)
        for kvh in range(N_KV_HEADS):
            kblk = kv_buf[r0:r0 + SPAN, kvh * HEAD_DIM:(kvh + 1) * HEAD_DIM]
            vblk = kv_buf[r0:r0 + SPAN, k_v + kvh * HEAD_DIM:k_v + (kvh + 1) * HEAD_DIM]
            for g in range(GROUP):
                h = kvh * GROUP + g
                qh = q_ref[r0:r0 + BLOCK, h * HEAD_DIM:(h + 1) * HEAD_DIM]
                sc = lax.dot_general(qh, kblk, (((1,), (1,)), ((), ())), preferred_element_type=_F32)
                sc = jnp.where(valid, sc - _ALIBI_SLOPES[h] * dist, NEG_INF)
                sink = sink_ref[h]
                m = jnp.maximum(jnp.max(sc, axis=-1, keepdims=True), sink)
                p = jnp.exp(sc - m)
                denom = jnp.sum(p, axis=-1, keepdims=True) + jnp.exp(sink - m)
                o = _dot(p.astype(_BF16), vblk) / denom
                o_ref[r0:r0 + BLOCK, h * HEAD_DIM:(h + 1) * HEAD_DIM] = o.astype(_BF16)


def _attention(q, kv, sink, seq_len):
    batch = q.shape[0]
    tiles = seq_len // ATTN_TILE
    halo_per_tile = ATTN_TILE // WINDOW
    last_halo = seq_len // WINDOW - 1
    return pl.pallas_call(
        functools.partial(_attn_kernel, seq_len),
        grid=(batch, tiles),
        in_specs=[
            pl.BlockSpec(memory_space=pltpu.SMEM),
            pl.BlockSpec((None, ATTN_TILE, Q_COLS), lambda b, i: (b, i, 0)),
            pl.BlockSpec((None, WINDOW, KV_COLS), lambda b, i: (b, jnp.maximum(i * halo_per_tile - 1, 0), 0)),
            pl.BlockSpec((None, ATTN_TILE, KV_COLS), lambda b, i: (b, i, 0)),
            pl.BlockSpec((None, WINDOW, KV_COLS), lambda b, i: (b, jnp.minimum((i + 1) * halo_per_tile, last_halo), 0)),
        ],
        out_specs=pl.BlockSpec((None, ATTN_TILE, Q_COLS), lambda b, i: (b, i, 0)),
        out_shape=jax.ShapeDtypeStruct((batch, seq_len, Q_COLS), _BF16),
        scratch_shapes=[pltpu.VMEM((ATTN_TILE + 2 * WINDOW, KV_COLS), _BF16)],
        compiler_params=_params("parallel", "parallel"),
        name="banded_gqa",
    )(sink, q, kv, kv, kv)


def _proj_residual_kernel(a_ref, x_ref, w_ref, g_ref, o_ref):
    y = _dot(a_ref[...], w_ref[...])
    o_ref[...] = x_ref[...] + _rmsnorm(y, g_ref[...])


def _proj_residual(a2d, x2d, w, g):
    rows = x2d.shape[0]
    return pl.pallas_call(
        _proj_residual_kernel,
        grid=(rows // ROW_TILE,),
        in_specs=[
            pl.BlockSpec((ROW_TILE, Q_COLS), lambda i: (i, 0)),
            pl.BlockSpec((ROW_TILE, D_MODEL), lambda i: (i, 0)),
            _resident((Q_COLS, D_MODEL)),
            _resident((1, D_MODEL)),
        ],
        out_specs=pl.BlockSpec((ROW_TILE, D_MODEL), lambda i: (i, 0)),
        out_shape=jax.ShapeDtypeStruct((rows, D_MODEL), _F32),
        compiler_params=_params("parallel"),
        name="attn_out_proj",
    )(a2d, x2d, w, g)


def _ffn_kernel(x_ref, gpre_ref, wg_ref, wu_ref, wd_ref, gpost_ref, o_ref):
    x = x_ref[...]
    h = _rmsnorm(x, gpre_ref[...]).astype(_BF16)
    acc = None
    c0 = 0
    for width in FF_CHUNKS:
        gate = _dot(h, wg_ref[:, c0:c0 + width])
        up = _dot(h, wu_ref[:, c0:c0 + width])
        act = (gate * _sigmoid(gate) * up).astype(_BF16)
        part = _dot(act, wd_ref[c0:c0 + width, :])
        acc = part if acc is None else acc + part
        c0 += width
    o_ref[...] = x + _rmsnorm(acc, gpost_ref[...])


def _ffn(x2d, g_pre, w_gate, w_up, w_down, g_post):
    rows = x2d.shape[0]
    return pl.pallas_call(
        _ffn_kernel,
        grid=(rows // ROW_TILE,),
        in_specs=[
            pl.BlockSpec((ROW_TILE, D_MODEL), lambda i: (i, 0)),
            _resident((1, D_MODEL)),
            _resident((D_MODEL, D_FF)),
            _resident((D_MODEL, D_FF)),
            _resident((D_FF, D_MODEL)),
            _resident((1, D_MODEL)),
        ],
        out_specs=pl.BlockSpec((ROW_TILE, D_MODEL), lambda i: (i, 0)),
        out_shape=jax.ShapeDtypeStruct((rows, D_MODEL), _F32),
        compiler_params=_params("parallel"),
        name="swiglu_ffn",
    )(x2d, g_pre, w_gate, w_up, w_down, g_post)


def _conv_kernel(xp_ref, xm_ref, xn_ref, gpre_ref, win_ref, bin_ref, wdw_ref, bdw_ref,
                 lng_ref, lnb_ref, wout_ref, bout_ref, gpost_ref, o_ref, glu_buf):
    i = pl.program_id(1)
    last = pl.num_programs(1) - 1

    def glu(x):
        h = _rmsnorm(x, gpre_ref[...]).astype(_BF16)
        y = _dot(h, win_ref[...]) + bin_ref[...]
        return y[:, :D_MODEL] * _sigmoid(y[:, D_MODEL:])

    x = xm_ref[...]
    glu_buf[0:CONV_HALO, :] = jnp.where(i > 0, glu(xp_ref[...]), 0.0)
    glu_buf[CONV_HALO:CONV_HALO + ROW_TILE, :] = glu(x)
    glu_buf[CONV_HALO + ROW_TILE:, :] = jnp.where(i < last, glu(xn_ref[...]), 0.0)

    acc = None
    for k in range(CONV_WIDTH):
        start = CONV_HALO - CONV_PAD + k
        term = glu_buf[start:start + ROW_TILE, :] * wdw_ref[k:k + 1, :]
        acc = term if acc is None else acc + term
    c = acc + bdw_ref[...]

    mu = jnp.mean(c, axis=-1, keepdims=True)
    cc = c - mu
    var = jnp.mean(cc * cc, axis=-1, keepdims=True)
    y = cc * lax.rsqrt(var + EPS) * lng_ref[...] + lnb_ref[...]
    y = (y * _sigmoid(y)).astype(_BF16)
    out = _dot(y, wout_ref[...]) + bout_ref[...]
    o_ref[...] = x + _rmsnorm(out, gpost_ref[...])


def _conformer_conv(x, g_pre, w_in, b_in, w_dw, b_dw, ln_g, ln_b, w_out, b_out, g_post):
    batch, seq_len, _ = x.shape
    tiles = seq_len // ROW_TILE
    halo_per_tile = ROW_TILE // CONV_HALO
    last_halo = seq_len // CONV_HALO - 1
    return pl.pallas_call(
        _conv_kernel,
        grid=(batch, tiles),
        in_specs=[
            pl.BlockSpec((None, CONV_HALO, D_MODEL), lambda b, i: (b, jnp.maximum(i * halo_per_tile - 1, 0), 0)),
            pl.BlockSpec((None, ROW_TILE, D_MODEL), lambda b, i: (b, i, 0)),
            pl.BlockSpec((None, CONV_HALO, D_MODEL), lambda b, i: (b, jnp.minimum((i + 1) * halo_per_tile, last_halo), 0)),
            _resident((1, D_MODEL)),
            _resident((D_MODEL, 2 * D_MODEL)),
            _resident((1, 2 * D_MODEL)),
            _resident((CONV_WIDTH, D_MODEL)),
            _resident((1, D_MODEL)),
            _resident((1, D_MODEL)),
            _resident((1, D_MODEL)),
            _resident((D_MODEL, D_MODEL)),
            _resident((1, D_MODEL)),
            _resident((1, D_MODEL)),
        ],
        out_specs=pl.BlockSpec((None, ROW_TILE, D_MODEL), lambda b, i: (b, i, 0)),
        out_shape=jax.ShapeDtypeStruct((batch, seq_len, D_MODEL), _F32),
        scratch_shapes=[pltpu.VMEM((ROW_TILE + 2 * CONV_HALO, D_MODEL), _F32)],
        compiler_params=_params("parallel", "parallel"),
        name="conformer_conv",
    )(x, x, x, g_pre, w_in, b_in, w_dw, b_dw, ln_g, ln_b, w_out, b_out, g_post)


def _row(v):
    return v.reshape(1, -1)


def _trunk(x, p):
    batch, seq_len, _ = x.shape
    rows = batch * seq_len
    for layer in range(DEPTH):
        li = layer // N_MIXERS
        if layer % N_MIXERS == 0:
            q, kv = _qkv_proj(x.reshape(rows, D_MODEL), _row(p["g_mix_pre"][layer]), p["w_qkv"][li])
            a = _attention(q.reshape(batch, seq_len, Q_COLS), kv.reshape(batch, seq_len, KV_COLS),
                           p["attn_sink"][li], seq_len)
            x2d = _proj_residual(a.reshape(rows, Q_COLS), x.reshape(rows, D_MODEL), p["w_o"][li],
                                 _row(p["g_mix_post"][layer]))
        else:
            x = _conformer_conv(x, _row(p["g_mix_pre"][layer]), p["conv_w_in"][li], _row(p["conv_b_in"][li]),
                                p["conv_w_dw"][li], _row(p["conv_b_dw"][li]), _row(p["conv_ln_g"][li]),
                                _row(p["conv_ln_b"][li]), p["conv_w_out"][li], _row(p["conv_b_out"][li]),
                                _row(p["g_mix_post"][layer]))
            x2d = x.reshape(rows, D_MODEL)
        x2d = _ffn(x2d, _row(p["g_ffn_pre"][layer]), p["ffn_w_gate"][layer], p["ffn_w_up"][layer],
                   p["ffn_w_down"][layer], _row(p["g_ffn_post"][layer]))
        x = x2d.reshape(batch, seq_len, D_MODEL)
    return x


def kernel(x_prompt, x_sample, g_mix_pre, g_mix_post, g_ffn_pre, g_ffn_post, w_qkv, w_o, attn_sink,
           conv_w_in, conv_b_in, conv_w_dw, conv_b_dw, conv_ln_g, conv_ln_b, conv_w_out, conv_b_out,
           ffn_w_gu, ffn_w_down):
    params = dict(
        g_mix_pre=g_mix_pre, g_mix_post=g_mix_post, g_ffn_pre=g_ffn_pre, g_ffn_post=g_ffn_post,
        w_qkv=w_qkv.astype(_BF16), w_o=w_o.astype(_BF16), attn_sink=attn_sink,
        conv_w_in=conv_w_in.astype(_BF16), conv_b_in=conv_b_in, conv_w_dw=conv_w_dw, conv_b_dw=conv_b_dw,
        conv_ln_g=conv_ln_g, conv_ln_b=conv_ln_b, conv_w_out=conv_w_out.astype(_BF16), conv_b_out=conv_b_out,
        ffn_w_gate=ffn_w_gu[:, :, :D_FF].astype(_BF16), ffn_w_up=ffn_w_gu[:, :, D_FF:].astype(_BF16),
        ffn_w_down=ffn_w_down.astype(_BF16),
    )
    return (_trunk(x_prompt, params), _trunk(x_sample, params))
```

```python
import functools
import math

import jax
import jax.numpy as jnp
import numpy as np
from jax import lax
from jax.experimental import pallas as pl
from jax.experimental.pallas import tpu as pltpu

D_MODEL = 1024
DEPTH = 4
N_MIXERS = 2
N_HEADS = 16
N_KV_HEADS = 4
HEAD_DIM = D_MODEL // N_HEADS
GROUP = N_HEADS // N_KV_HEADS
WINDOW = 128
BLOCK = 128
SPAN = BLOCK + 2 * WINDOW
CONV_WIDTH = 31
CONV_PAD = CONV_WIDTH // 2
D_FF = -(-8 * D_MODEL // (3 * 256)) * 256
EPS = 1e-6
NEG_INF = -1e30
LOG2E = math.log2(math.e)

Q_COLS = N_HEADS * HEAD_DIM
KV_HEAD_COLS = N_KV_HEADS * HEAD_DIM
GROUP_COLS = GROUP * BLOCK
CONV_HALO = 16
LANES = 128

V7X_VMEM_BYTES = 64 * 1024 * 1024
VMEM_LIMIT_BYTES = 60000 * 1024
assert VMEM_LIMIT_BYTES < V7X_VMEM_BYTES

ROW_TILE = 512
ATTN_TILE = 512
FF_CHUNKS = (768, 768, 768, 512)
assert sum(FF_CHUNKS) == D_FF

_BF16 = jnp.bfloat16
_F32 = jnp.float32


def _params(*semantics):
    return pltpu.CompilerParams(dimension_semantics=semantics, vmem_limit_bytes=VMEM_LIMIT_BYTES)


def _resident(shape):
    zeros = (0,) * len(shape)
    return pl.BlockSpec(shape, lambda *_: zeros, pipeline_mode=pl.Buffered(1))


def _rmsnorm(x, g):
    return x * lax.rsqrt(jnp.mean(x * x, axis=-1, keepdims=True) + EPS) * g


def _sigmoid(x):
    return 1.0 / (1.0 + jnp.exp(-x))


def _dot(a, b):
    return jnp.dot(a, b, preferred_element_type=_F32)


def _dot_nt(a, b):
    return lax.dot_general(a, b, (((1,), (1,)), ((), ())), preferred_element_type=_F32)


def _dot_tn(a, b):
    return lax.dot_general(a, b, (((0,), (0,)), ((), ())), preferred_element_type=_F32)


def _qkv_kernel(x_ref, g_ref, wqt_ref, wk_ref, wvt_ref, qt_ref, k_ref, vt_ref):
    h = _rmsnorm(x_ref[...], g_ref[...]).astype(_BF16)
    qt_ref[...] = (_dot_nt(wqt_ref[...], h) * (HEAD_DIM ** -0.5 * LOG2E)).astype(_BF16)
    k_ref[...] = _dot(h, wk_ref[...]).astype(_BF16)
    vt_ref[...] = _dot_nt(wvt_ref[...], h).astype(_BF16)


def _qkv_proj(x, g, w_qt, w_k, w_vt):
    batch, seq_len, _ = x.shape
    return pl.pallas_call(
        _qkv_kernel,
        grid=(batch, seq_len // ROW_TILE),
        in_specs=[
            pl.BlockSpec((None, ROW_TILE, D_MODEL), lambda b, i: (b, i, 0)),
            _resident((1, D_MODEL)),
            _resident((Q_COLS, D_MODEL)),
            _resident((D_MODEL, KV_HEAD_COLS)),
            _resident((KV_HEAD_COLS, D_MODEL)),
        ],
        out_specs=[
            pl.BlockSpec((None, Q_COLS, ROW_TILE), lambda b, i: (b, 0, i)),
            pl.BlockSpec((None, ROW_TILE, KV_HEAD_COLS), lambda b, i: (b, i, 0)),
            pl.BlockSpec((None, KV_HEAD_COLS, ROW_TILE), lambda b, i: (b, 0, i)),
        ],
        out_shape=[
            jax.ShapeDtypeStruct((batch, Q_COLS, seq_len), _BF16),
            jax.ShapeDtypeStruct((batch, seq_len, KV_HEAD_COLS), _BF16),
            jax.ShapeDtypeStruct((batch, KV_HEAD_COLS, seq_len), _BF16),
        ],
        compiler_params=_params("parallel", "parallel"),
        name="qkv_proj",
    )(x, g, w_qt, w_k, w_vt)


def _alibi_table():
    s = np.arange(SPAN, dtype=np.int64)[:, None]
    i = np.arange(BLOCK, dtype=np.int64)[None, :]
    dist = np.abs(i - s + WINDOW).astype(np.float32)
    heads = np.arange(1, N_HEADS + 1, dtype=np.float32)
    slopes = np.power(2.0, -8.0 * heads / N_HEADS).astype(np.float32)
    table = (-slopes[:, None, None] * dist[None]) * np.float32(LOG2E)
    table = table.reshape(N_KV_HEADS, GROUP, SPAN, BLOCK).transpose(0, 2, 1, 3)
    return np.ascontiguousarray(table.reshape(N_KV_HEADS, SPAN, GROUP_COLS)).astype(np.float32)


def _attn_kernel(qt_ref, kp_ref, km_ref, kn_ref, vtp_ref, vtm_ref, vtn_ref, alibi_ref, sink_ref,
                 x_ref, wo_ref, g_ref, o_ref, k_buf, vt_buf, ot_buf):
    i = pl.program_id(1)
    last = pl.num_programs(1) - 1
    k_buf[0:WINDOW, :] = kp_ref[...]
    k_buf[WINDOW:WINDOW + ATTN_TILE, :] = km_ref[...]
    k_buf[WINDOW + ATTN_TILE:, :] = kn_ref[...]
    vt_buf[:, 0:WINDOW] = vtp_ref[...]
    vt_buf[:, WINDOW:WINDOW + ATTN_TILE] = vtm_ref[...]
    vt_buf[:, WINDOW + ATTN_TILE:] = vtn_ref[...]
    key = lax.broadcasted_iota(jnp.int32, (BLOCK, GROUP_COLS), 0)
    qry = lax.broadcasted_iota(jnp.int32, (BLOCK, GROUP_COLS), 1) & (BLOCK - 1)
    left_band = key >= qry
    right_band = key <= qry

    for jb in range(ATTN_TILE // BLOCK):
        r0 = jb * BLOCK
        left_ok = left_band if jb > 0 else jnp.logical_and(left_band, i > 0)
        right_ok = right_band if jb < ATTN_TILE // BLOCK - 1 else jnp.logical_and(right_band, i < last)
        for kvh in range(N_KV_HEADS):
            lo = kvh * HEAD_DIM
            h0 = kvh * GROUP * HEAD_DIM
            q_heads = jnp.concatenate(
                [qt_ref[h0 + g * HEAD_DIM:h0 + (g + 1) * HEAD_DIM, r0:r0 + BLOCK] for g in range(GROUP)], axis=1)
            pads = (lo, KV_HEAD_COLS - lo - HEAD_DIM)
            above, below = [[jnp.zeros((n, GROUP_COLS), _BF16)] if n else [] for n in pads]
            q_rhs = jnp.concatenate(above + [q_heads] + below, axis=0)
            st = _dot(k_buf[r0:r0 + SPAN, :], q_rhs)
            bias = alibi_ref[kvh]
            sl = jnp.where(left_ok, st[0:BLOCK] + bias[0:BLOCK], NEG_INF)
            sc = st[BLOCK:2 * BLOCK] + bias[BLOCK:2 * BLOCK]
            sr = jnp.where(right_ok, st[2 * BLOCK:] + bias[2 * BLOCK:], NEG_INF)
            sink = sink_ref[kvh]
            m = jnp.maximum(jnp.maximum(jnp.max(sl, axis=0, keepdims=True), jnp.max(sc, axis=0, keepdims=True)),
                            jnp.maximum(jnp.max(sr, axis=0, keepdims=True), sink))
            pl_, pc, pr = jnp.exp2(sl - m), jnp.exp2(sc - m), jnp.exp2(sr - m)
            denom = (jnp.sum(pl_, axis=0, keepdims=True) + jnp.sum(pc, axis=0, keepdims=True)
                     + jnp.sum(pr, axis=0, keepdims=True) + jnp.exp2(sink - m))
            p = jnp.concatenate([pl_, pc, pr], axis=0).astype(_BF16)
            ot = _dot(vt_buf[lo:lo + HEAD_DIM, r0:r0 + SPAN], p) * (1.0 / denom)
            for g in range(GROUP):
                h0 = (kvh * GROUP + g) * HEAD_DIM
                ot_buf[h0:h0 + HEAD_DIM, r0:r0 + BLOCK] = ot[:, g * BLOCK:(g + 1) * BLOCK].astype(_BF16)

    y = _dot_tn(ot_buf[...], wo_ref[...])
    o_ref[...] = x_ref[...] + _rmsnorm(y, g_ref[...])


def _attention(qt, k, vt, alibi, sink_rows, x, w_o, g_post):
    batch, seq_len, _ = x.shape
    halo_per_tile = ATTN_TILE // WINDOW
    last_halo = seq_len // WINDOW - 1

    def prev_blk(i):
        return jnp.maximum(i * halo_per_tile - 1, 0)

    def next_blk(i):
        return jnp.minimum((i + 1) * halo_per_tile, last_halo)

    return pl.pallas_call(
        _attn_kernel,
        grid=(batch, seq_len // ATTN_TILE),
        in_specs=[
            pl.BlockSpec((None, Q_COLS, ATTN_TILE), lambda b, i: (b, 0, i)),
            pl.BlockSpec((None, WINDOW, KV_HEAD_COLS), lambda b, i: (b, prev_blk(i), 0)),
            pl.BlockSpec((None, ATTN_TILE, KV_HEAD_COLS), lambda b, i: (b, i, 0)),
            pl.BlockSpec((None, WINDOW, KV_HEAD_COLS), lambda b, i: (b, next_blk(i), 0)),
            pl.BlockSpec((None, KV_HEAD_COLS, WINDOW), lambda b, i: (b, 0, prev_blk(i))),
            pl.BlockSpec((None, KV_HEAD_COLS, ATTN_TILE), lambda b, i: (b, 0, i)),
            pl.BlockSpec((None, KV_HEAD_COLS, WINDOW), lambda b, i: (b, 0, next_blk(i))),
            _resident((N_KV_HEADS, SPAN, GROUP_COLS)),
            _resident((N_KV_HEADS, 1, GROUP_COLS)),
            pl.BlockSpec((None, ATTN_TILE, D_MODEL), lambda b, i: (b, i, 0)),
            _resident((Q_COLS, D_MODEL)),
            _resident((1, D_MODEL)),
        ],
        out_specs=pl.BlockSpec((None, ATTN_TILE, D_MODEL), lambda b, i: (b, i, 0)),
        out_shape=jax.ShapeDtypeStruct((batch, seq_len, D_MODEL), _F32),
        scratch_shapes=[
            pltpu.VMEM((ATTN_TILE + 2 * WINDOW, KV_HEAD_COLS), _BF16),
            pltpu.VMEM((KV_HEAD_COLS, ATTN_TILE + 2 * WINDOW), _BF16),
            pltpu.VMEM((Q_COLS, ATTN_TILE), _BF16),
        ],
        compiler_params=_params("parallel", "parallel"),
        name="banded_gqa",
    )(qt, k, k, k, vt, vt, vt, alibi, sink_rows, x, w_o, g_post)


def _ffn_kernel(x_ref, gpre_ref, wg_ref, wu_ref, wd_ref, gpost_ref, o_ref):
    x = x_ref[...]
    h = _rmsnorm(x, gpre_ref[...]).astype(_BF16)
    acc = None
    c0 = 0
    for width in FF_CHUNKS:
        gate = _dot(h, wg_ref[:, c0:c0 + width])
        up = _dot(h, wu_ref[:, c0:c0 + width])
        act = (gate * _sigmoid(gate) * up).astype(_BF16)
        part = _dot(act, wd_ref[c0:c0 + width, :])
        acc = part if acc is None else acc + part
        c0 += width
    o_ref[...] = x + _rmsnorm(acc, gpost_ref[...])


def _ffn(x2d, g_pre, w_gate, w_up, w_down, g_post):
    rows = x2d.shape[0]
    return pl.pallas_call(
        _ffn_kernel,
        grid=(rows // ROW_TILE,),
        in_specs=[
            pl.BlockSpec((ROW_TILE, D_MODEL), lambda i: (i, 0)),
            _resident((1, D_MODEL)),
            _resident((D_MODEL, D_FF)),
            _resident((D_MODEL, D_FF)),
            _resident((D_FF, D_MODEL)),
            _resident((1, D_MODEL)),
        ],
        out_specs=pl.BlockSpec((ROW_TILE, D_MODEL), lambda i: (i, 0)),
        out_shape=jax.ShapeDtypeStruct((rows, D_MODEL), _F32),
        compiler_params=_params("parallel"),
        name="swiglu_ffn",
    )(x2d, g_pre, w_gate, w_up, w_down, g_post)


_SLABS = D_MODEL // LANES
_HALF = ROW_TILE // 2


def _conv_kernel(xp_ref, xm_ref, xn_ref, gpre_ref, win_ref, bin_ref, wdw_ref, bdw_ref,
                 lng_ref, lnb_ref, wout_ref, bout_ref, gpost_ref, o_ref, glu_buf, conv_buf):
    i = pl.program_id(1)
    last = pl.num_programs(1) - 1

    x = xm_ref[...]
    x_all = jnp.concatenate([xp_ref[...], x, xn_ref[...]], axis=0)
    h = _rmsnorm(x_all, gpre_ref[...]).astype(_BF16)
    y = _dot(h, win_ref[...]) + bin_ref[...]
    glu = y[:, :D_MODEL] * _sigmoid(y[:, D_MODEL:])
    for c in range(_SLABS):
        glu_buf[c] = glu[:, c * LANES:(c + 1) * LANES]

    @pl.when(i == 0)
    def _():
        glu_buf[:, 0:CONV_HALO, :] = jnp.zeros((_SLABS, CONV_HALO, LANES), _F32)

    @pl.when(i == last)
    def _():
        glu_buf[:, CONV_HALO + ROW_TILE:, :] = jnp.zeros((_SLABS, CONV_HALO, LANES), _F32)

    for c in range(_SLABS):
        lanes = slice(c * LANES, (c + 1) * LANES)
        for parity in range(2):
            acc = None
            for k in range(CONV_WIDTH):
                start = CONV_HALO - CONV_PAD + parity + k
                term = glu_buf[c, pl.ds(start, _HALF, stride=2), :] * wdw_ref[k:k + 1, lanes]
                acc = term if acc is None else acc + term
            conv_buf[c, pl.ds(parity, _HALF, stride=2), :] = acc + bdw_ref[:, lanes]

    total = conv_buf[0]
    for c in range(1, _SLABS):
        total = total + conv_buf[c]
    mu = jnp.sum(total, axis=-1, keepdims=True) * (1.0 / D_MODEL)
    sq = None
    for c in range(_SLABS):
        d = conv_buf[c] - mu
        sq = d * d if sq is None else sq + d * d
    rstd = lax.rsqrt(jnp.sum(sq, axis=-1, keepdims=True) * (1.0 / D_MODEL) + EPS)
    acts = []
    for c in range(_SLABS):
        lanes = slice(c * LANES, (c + 1) * LANES)
        z = (conv_buf[c] - mu) * rstd * lng_ref[:, lanes] + lnb_ref[:, lanes]
        acts.append((z * _sigmoid(z)).astype(_BF16))
    out = _dot(jnp.concatenate(acts, axis=-1), wout_ref[...]) + bout_ref[...]
    o_ref[...] = x + _rmsnorm(out, gpost_ref[...])


def _conformer_conv(x, g_pre, w_in, b_in, w_dw, b_dw, ln_g, ln_b, w_out, b_out, g_post):
    batch, seq_len, _ = x.shape
    tiles = seq_len // ROW_TILE
    halo_per_tile = ROW_TILE // CONV_HALO
    last_halo = seq_len // CONV_HALO - 1
    return pl.pallas_call(
        _conv_kernel,
        grid=(batch, tiles),
        in_specs=[
            pl.BlockSpec((None, CONV_HALO, D_MODEL), lambda b, i: (b, jnp.maximum(i * halo_per_tile - 1, 0), 0)),
            pl.BlockSpec((None, ROW_TILE, D_MODEL), lambda b, i: (b, i, 0)),
            pl.BlockSpec((None, CONV_HALO, D_MODEL), lambda b, i: (b, jnp.minimum((i + 1) * halo_per_tile, last_halo), 0)),
            _resident((1, D_MODEL)),
            _resident((D_MODEL, 2 * D_MODEL)),
            _resident((1, 2 * D_MODEL)),
            _resident((CONV_WIDTH, D_MODEL)),
            _resident((1, D_MODEL)),
            _resident((1, D_MODEL)),
            _resident((1, D_MODEL)),
            _resident((D_MODEL, D_MODEL)),
            _resident((1, D_MODEL)),
            _resident((1, D_MODEL)),
        ],
        out_specs=pl.BlockSpec((None, ROW_TILE, D_MODEL), lambda b, i: (b, i, 0)),
        out_shape=jax.ShapeDtypeStruct((batch, seq_len, D_MODEL), _F32),
        scratch_shapes=[
            pltpu.VMEM((_SLABS, ROW_TILE + 2 * CONV_HALO, LANES), _F32),
            pltpu.VMEM((_SLABS, ROW_TILE, LANES), _F32),
        ],
        compiler_params=_params("parallel", "parallel"),
        name="conformer_conv",
    )(x, x, x, g_pre, w_in, b_in, w_dw, b_dw, ln_g, ln_b, w_out, b_out, g_post)


def _row(v):
    return v.reshape(1, -1)


def _trunk(x, p):
    batch, seq_len, _ = x.shape
    rows = batch * seq_len
    for layer in range(DEPTH):
        li = layer // N_MIXERS
        if layer % N_MIXERS == 0:
            qt, k, vt = _qkv_proj(x, _row(p["g_mix_pre"][layer]), p["w_qt"][li], p["w_k"][li], p["w_vt"][li])
            x = _attention(qt, k, vt, p["alibi"], p["sink_rows"][li], x, p["w_o"][li], _row(p["g_mix_post"][layer]))
        else:
            x = _conformer_conv(x, _row(p["g_mix_pre"][layer]), p["conv_w_in"][li], _row(p["conv_b_in"][li]),
                                p["conv_w_dw"][li], _row(p["conv_b_dw"][li]), _row(p["conv_ln_g"][li]),
                                _row(p["conv_ln_b"][li]), p["conv_w_out"][li], _row(p["conv_b_out"][li]),
                                _row(p["g_mix_post"][layer]))
        x2d = _ffn(x.reshape(rows, D_MODEL), _row(p["g_ffn_pre"][layer]), p["ffn_w_gate"][layer],
                   p["ffn_w_up"][layer], p["ffn_w_down"][layer], _row(p["g_ffn_post"][layer]))
        x = x2d.reshape(batch, seq_len, D_MODEL)
    return x


def kernel(x_prompt, x_sample, g_mix_pre, g_mix_post, g_ffn_pre, g_ffn_post, w_qkv, w_o, attn_sink,
           conv_w_in, conv_b_in, conv_w_dw, conv_b_dw, conv_ln_g, conv_ln_b, conv_w_out, conv_b_out,
           ffn_w_gu, ffn_w_down):
    n_layers = attn_sink.shape[0]
    sink_rows = jnp.repeat((attn_sink * LOG2E).reshape(n_layers, N_KV_HEADS, 1, GROUP), BLOCK, axis=-1)
    params = dict(
        g_mix_pre=g_mix_pre, g_mix_post=g_mix_post, g_ffn_pre=g_ffn_pre, g_ffn_post=g_ffn_post,
        w_qt=jnp.swapaxes(w_qkv[:, :, :Q_COLS], 1, 2).astype(_BF16),
        w_k=w_qkv[:, :, Q_COLS:Q_COLS + KV_HEAD_COLS].astype(_BF16),
        w_vt=jnp.swapaxes(w_qkv[:, :, Q_COLS + KV_HEAD_COLS:], 1, 2).astype(_BF16),
        w_o=w_o.astype(_BF16), alibi=jnp.asarray(_alibi_table()), sink_rows=sink_rows,
        conv_w_in=conv_w_in.astype(_BF16), conv_b_in=conv_b_in, conv_w_dw=conv_w_dw, conv_b_dw=conv_b_dw,
        conv_ln_g=conv_ln_g, conv_ln_b=conv_ln_b, conv_w_out=conv_w_out.astype(_BF16), conv_b_out=conv_b_out,
        ffn_w_gate=ffn_w_gu[:, :, :D_FF].astype(_BF16), ffn_w_up=ffn_w_gu[:, :, D_FF:].astype(_BF16),
        ffn_w_down=ffn_w_down.astype(_BF16),
    )
    return (_trunk(x_prompt, params), _trunk(x_sample, params))
```

```python
import functools
import math

import jax
import jax.numpy as jnp
import numpy as np
from jax import lax
from jax.experimental import pallas as pl
from jax.experimental.pallas import tpu as pltpu

D_MODEL = 1024
DEPTH = 4
N_MIXERS = 2
N_HEADS = 16
N_KV_HEADS = 4
HEAD_DIM = D_MODEL // N_HEADS
GROUP = N_HEADS // N_KV_HEADS
WINDOW = 128
BLOCK = 128
SPAN = BLOCK + 2 * WINDOW
CONV_WIDTH = 31
CONV_PAD = CONV_WIDTH // 2
D_FF = -(-8 * D_MODEL // (3 * 256)) * 256
EPS = 1e-6
NEG_INF = -1e30
LOG2E = math.log2(math.e)

Q_COLS = N_HEADS * HEAD_DIM
KV_HEAD_COLS = N_KV_HEADS * HEAD_DIM
GROUP_COLS = GROUP * BLOCK
CONV_HALO = 16
LANES = 128

V7X_VMEM_BYTES = 64 * 1024 * 1024
VMEM_LIMIT_BYTES = 60000 * 1024
assert VMEM_LIMIT_BYTES < V7X_VMEM_BYTES

ROW_TILE = 512
ATTN_TILE = 512
FF_CHUNKS = (768, 768, 768, 512)
assert sum(FF_CHUNKS) == D_FF

_BF16 = jnp.bfloat16
_F32 = jnp.float32


def _params(*semantics):
    return pltpu.CompilerParams(dimension_semantics=semantics, vmem_limit_bytes=VMEM_LIMIT_BYTES)


def _resident(shape):
    zeros = (0,) * len(shape)
    return pl.BlockSpec(shape, lambda *_: zeros, pipeline_mode=pl.Buffered(1))


def _rmsnorm(x, g):
    return x * lax.rsqrt(jnp.mean(x * x, axis=-1, keepdims=True) + EPS) * g


def _sigmoid(x):
    return 1.0 / (1.0 + jnp.exp(-x))


def _dot(a, b):
    return jnp.dot(a, b, preferred_element_type=_F32)


def _dot_nt(a, b):
    return lax.dot_general(a, b, (((1,), (1,)), ((), ())), preferred_element_type=_F32)


def _dot_tn(a, b):
    return lax.dot_general(a, b, (((0,), (0,)), ((), ())), preferred_element_type=_F32)


def _qkv_kernel(x_ref, g_ref, wqt_ref, wk_ref, wvt_ref, qt_ref, k_ref, vt_ref):
    h = _rmsnorm(x_ref[...], g_ref[...]).astype(_BF16)
    qt = (_dot_nt(wqt_ref[...], h) * (HEAD_DIM ** -0.5 * LOG2E)).astype(_BF16)
    for j in range(ROW_TILE // BLOCK):
        qt_ref[j] = qt[:, j * BLOCK:(j + 1) * BLOCK]
    k_ref[...] = _dot(h, wk_ref[...]).astype(_BF16)
    vt_ref[...] = _dot_nt(wvt_ref[...], h).astype(_BF16)


def _qkv_proj(x, g, w_qt, w_k, w_vt):
    batch, seq_len, _ = x.shape
    return pl.pallas_call(
        _qkv_kernel,
        grid=(batch, seq_len // ROW_TILE),
        in_specs=[
            pl.BlockSpec((None, ROW_TILE, D_MODEL), lambda b, i: (b, i, 0)),
            _resident((1, D_MODEL)),
            _resident((Q_COLS, D_MODEL)),
            _resident((D_MODEL, KV_HEAD_COLS)),
            _resident((KV_HEAD_COLS, D_MODEL)),
        ],
        out_specs=[
            pl.BlockSpec((None, ROW_TILE // BLOCK, Q_COLS, BLOCK), lambda b, i: (b, i, 0, 0)),
            pl.BlockSpec((None, ROW_TILE, KV_HEAD_COLS), lambda b, i: (b, i, 0)),
            pl.BlockSpec((None, KV_HEAD_COLS, ROW_TILE), lambda b, i: (b, 0, i)),
        ],
        out_shape=[
            jax.ShapeDtypeStruct((batch, seq_len // BLOCK, Q_COLS, BLOCK), _BF16),
            jax.ShapeDtypeStruct((batch, seq_len, KV_HEAD_COLS), _BF16),
            jax.ShapeDtypeStruct((batch, KV_HEAD_COLS, seq_len), _BF16),
        ],
        compiler_params=_params("parallel", "parallel"),
        name="qkv_proj",
    )(x, g, w_qt, w_k, w_vt)


def _alibi_table():
    s = np.arange(SPAN, dtype=np.int64)[:, None]
    i = np.arange(BLOCK, dtype=np.int64)[None, :]
    dist = np.abs(i - s + WINDOW).astype(np.float32)
    heads = np.arange(1, N_HEADS + 1, dtype=np.float32)
    slopes = np.power(2.0, -8.0 * heads / N_HEADS).astype(np.float32)
    table = (-slopes[:, None, None] * dist[None]) * np.float32(LOG2E)
    table = table.reshape(N_KV_HEADS, GROUP, SPAN, BLOCK).transpose(0, 2, 1, 3)
    return np.ascontiguousarray(table.reshape(N_KV_HEADS, SPAN, GROUP_COLS)).astype(np.float32)


_Q_BLOCKS = ATTN_TILE // BLOCK


def _attn_kernel(qt_ref, kp_ref, km_ref, kn_ref, vtp_ref, vtm_ref, vtn_ref, alibi_ref, sink_ref,
                 x_ref, wo_ref, g_ref, o_ref, k_buf, vt_buf, st_buf, ot_buf):
    i = pl.program_id(1)
    last = pl.num_programs(1) - 1
    k_buf[0:WINDOW, :] = kp_ref[...]
    k_buf[WINDOW:WINDOW + ATTN_TILE, :] = km_ref[...]
    k_buf[WINDOW + ATTN_TILE:, :] = kn_ref[...]
    vt_buf[0] = vtp_ref[...]
    for j in range(_Q_BLOCKS):
        vt_buf[1 + j] = vtm_ref[:, j * BLOCK:(j + 1) * BLOCK]
    vt_buf[_Q_BLOCKS + 1] = vtn_ref[...]

    def fold8(v, op):
        return op(v.reshape(BLOCK // 8, 8, v.shape[-1]), axis=0)

    def scores(jb, kvh, slot):
        lo = kvh * HEAD_DIM
        h0 = kvh * GROUP * HEAD_DIM
        q_heads = jnp.concatenate(
            [qt_ref[jb, h0 + g * HEAD_DIM:h0 + (g + 1) * HEAD_DIM, :] for g in range(GROUP)], axis=1)
        pads = (lo, KV_HEAD_COLS - lo - HEAD_DIM)
        above, below = [[jnp.zeros((n, GROUP_COLS), _BF16)] if n else [] for n in pads]
        q_rhs = jnp.concatenate(above + [q_heads] + below, axis=0)
        r0 = pl.multiple_of(jb * BLOCK, BLOCK)
        st_buf[slot] = _dot(k_buf[pl.ds(r0, SPAN), :], q_rhs)

    def softmax_pv(jb, kvh, slot):
        key = lax.broadcasted_iota(jnp.int32, (BLOCK, GROUP_COLS), 0)
        qry = lax.broadcasted_iota(jnp.int32, (1, GROUP_COLS), 1) & (BLOCK - 1)
        first_key = jnp.where(jnp.logical_or(jb > 0, i > 0), qry, BLOCK)
        last_key = jnp.where(jnp.logical_or(jb < _Q_BLOCKS - 1, i < last), qry, -1)
        lo = kvh * HEAD_DIM
        bias = alibi_ref[kvh]
        sl = jnp.where(key >= first_key, st_buf[slot, 0:BLOCK] + bias[0:BLOCK], NEG_INF)
        sc = st_buf[slot, BLOCK:2 * BLOCK] + bias[BLOCK:2 * BLOCK]
        sr = jnp.where(key <= last_key, st_buf[slot, 2 * BLOCK:] + bias[2 * BLOCK:], NEG_INF)
        sink = sink_ref[kvh]
        m8 = jnp.maximum(jnp.maximum(fold8(sl, jnp.max), fold8(sc, jnp.max)), fold8(sr, jnp.max))
        m = jnp.maximum(jnp.max(m8, axis=0, keepdims=True), sink)
        pl_, pc, pr = jnp.exp2(sl - m), jnp.exp2(sc - m), jnp.exp2(sr - m)
        d8 = fold8(pl_, jnp.sum) + fold8(pc, jnp.sum) + fold8(pr, jnp.sum)
        denom = jnp.sum(d8, axis=0, keepdims=True) + jnp.exp2(sink - m)
        p = jnp.concatenate([pl_, pc, pr], axis=0).astype(_BF16)
        vt = jnp.concatenate([vt_buf[jb + t, lo:lo + HEAD_DIM, :] for t in range(SPAN // BLOCK)], axis=1)
        ot = _dot(vt, p) * (1.0 / denom)
        for g in range(GROUP):
            h0 = (kvh * GROUP + g) * HEAD_DIM
            ot_buf[jb, h0:h0 + HEAD_DIM, :] = ot[:, g * BLOCK:(g + 1) * BLOCK].astype(_BF16)

    scores(0, 0, 0)

    def block_step(jb, carry):
        for kvh in range(N_KV_HEADS):
            if kvh + 1 < N_KV_HEADS:
                scores(jb, kvh + 1, (kvh + 1) % 2)
            else:
                scores(jnp.minimum(jb + 1, _Q_BLOCKS - 1), 0, 0)
            softmax_pv(jb, kvh, kvh % 2)
        return carry

    lax.fori_loop(0, _Q_BLOCKS, block_step, 0)

    ot_all = jnp.concatenate([ot_buf[j] for j in range(_Q_BLOCKS)], axis=1)
    o_ref[...] = x_ref[...] + _rmsnorm(_dot_tn(ot_all, wo_ref[...]), g_ref[...])


def _attention(qt, k, vt, alibi, sink_rows, x, w_o, g_post):
    batch, seq_len, _ = x.shape
    halo_per_tile = ATTN_TILE // WINDOW
    last_halo = seq_len // WINDOW - 1

    def prev_blk(i):
        return jnp.maximum(i * halo_per_tile - 1, 0)

    def next_blk(i):
        return jnp.minimum((i + 1) * halo_per_tile, last_halo)

    return pl.pallas_call(
        _attn_kernel,
        grid=(batch, seq_len // ATTN_TILE),
        in_specs=[
            pl.BlockSpec((None, _Q_BLOCKS, Q_COLS, BLOCK), lambda b, i: (b, i, 0, 0)),
            pl.BlockSpec((None, WINDOW, KV_HEAD_COLS), lambda b, i: (b, prev_blk(i), 0)),
            pl.BlockSpec((None, ATTN_TILE, KV_HEAD_COLS), lambda b, i: (b, i, 0)),
            pl.BlockSpec((None, WINDOW, KV_HEAD_COLS), lambda b, i: (b, next_blk(i), 0)),
            pl.BlockSpec((None, KV_HEAD_COLS, WINDOW), lambda b, i: (b, 0, prev_blk(i))),
            pl.BlockSpec((None, KV_HEAD_COLS, ATTN_TILE), lambda b, i: (b, 0, i)),
            pl.BlockSpec((None, KV_HEAD_COLS, WINDOW), lambda b, i: (b, 0, next_blk(i))),
            _resident((N_KV_HEADS, SPAN, GROUP_COLS)),
            _resident((N_KV_HEADS, 1, GROUP_COLS)),
            pl.BlockSpec((None, ATTN_TILE, D_MODEL), lambda b, i: (b, i, 0)),
            _resident((Q_COLS, D_MODEL)),
            _resident((1, D_MODEL)),
        ],
        out_specs=pl.BlockSpec((None, ATTN_TILE, D_MODEL), lambda b, i: (b, i, 0)),
        out_shape=jax.ShapeDtypeStruct((batch, seq_len, D_MODEL), _F32),
        scratch_shapes=[
            pltpu.VMEM((ATTN_TILE + 2 * WINDOW, KV_HEAD_COLS), _BF16),
            pltpu.VMEM((_Q_BLOCKS + 2, KV_HEAD_COLS, BLOCK), _BF16),
            pltpu.VMEM((2, SPAN, GROUP_COLS), _F32),
            pltpu.VMEM((_Q_BLOCKS, Q_COLS, BLOCK), _BF16),
        ],
        compiler_params=_params("parallel", "parallel"),
        name="banded_gqa",
    )(qt, k, k, k, vt, vt, vt, alibi, sink_rows, x, w_o, g_post)


def _ffn_kernel(x_ref, gpre_ref, wg_ref, wu_ref, wd_ref, gpost_ref, o_ref):
    x = x_ref[...]
    h = _rmsnorm(x, gpre_ref[...]).astype(_BF16)
    acc = None
    c0 = 0
    for width in FF_CHUNKS:
        gate = _dot(h, wg_ref[:, c0:c0 + width])
        up = _dot(h, wu_ref[:, c0:c0 + width])
        act = (gate * _sigmoid(gate) * up).astype(_BF16)
        part = _dot(act, wd_ref[c0:c0 + width, :])
        acc = part if acc is None else acc + part
        c0 += width
    o_ref[...] = x + _rmsnorm(acc, gpost_ref[...])


def _ffn(x2d, g_pre, w_gate, w_up, w_down, g_post):
    rows = x2d.shape[0]
    return pl.pallas_call(
        _ffn_kernel,
        grid=(rows // ROW_TILE,),
        in_specs=[
            pl.BlockSpec((ROW_TILE, D_MODEL), lambda i: (i, 0)),
            _resident((1, D_MODEL)),
            _resident((D_MODEL, D_FF)),
            _resident((D_MODEL, D_FF)),
            _resident((D_FF, D_MODEL)),
            _resident((1, D_MODEL)),
        ],
        out_specs=pl.BlockSpec((ROW_TILE, D_MODEL), lambda i: (i, 0)),
        out_shape=jax.ShapeDtypeStruct((rows, D_MODEL), _F32),
        compiler_params=_params("parallel"),
        name="swiglu_ffn",
    )(x2d, g_pre, w_gate, w_up, w_down, g_post)


_SLABS = D_MODEL // LANES
_HALF = ROW_TILE // 2


def _conv_kernel(xp_ref, xm_ref, xn_ref, gpre_ref, win_ref, bin_ref, wdw_ref, bdw_ref,
                 lng_ref, lnb_ref, wout_ref, bout_ref, gpost_ref, o_ref, glu_buf, conv_buf):
    i = pl.program_id(1)
    last = pl.num_programs(1) - 1

    x = xm_ref[...]
    x_all = jnp.concatenate([xp_ref[...], x, xn_ref[...]], axis=0)
    h = _rmsnorm(x_all, gpre_ref[...]).astype(_BF16)
    y = _dot(h, win_ref[...]) + bin_ref[...]
    glu = y[:, :D_MODEL] * _sigmoid(y[:, D_MODEL:])
    for c in range(_SLABS):
        glu_buf[c] = glu[:, c * LANES:(c + 1) * LANES]

    @pl.when(i == 0)
    def _():
        glu_buf[:, 0:CONV_HALO, :] = jnp.zeros((_SLABS, CONV_HALO, LANES), _F32)

    @pl.when(i == last)
    def _():
        glu_buf[:, CONV_HALO + ROW_TILE:, :] = jnp.zeros((_SLABS, CONV_HALO, LANES), _F32)

    for c in range(_SLABS):
        lanes = slice(c * LANES, (c + 1) * LANES)
        for parity in range(2):
            acc = None
            for k in range(CONV_WIDTH):
                start = CONV_HALO - CONV_PAD + parity + k
                term = glu_buf[c, pl.ds(start, _HALF, stride=2), :] * wdw_ref[k:k + 1, lanes]
                acc = term if acc is None else acc + term
            conv_buf[c, pl.ds(parity, _HALF, stride=2), :] = acc + bdw_ref[:, lanes]

    total = conv_buf[0]
    for c in range(1, _SLABS):
        total = total + conv_buf[c]
    mu = jnp.sum(total, axis=-1, keepdims=True) * (1.0 / D_MODEL)
    sq = None
    for c in range(_SLABS):
        d = conv_buf[c] - mu
        sq = d * d if sq is None else sq + d * d
    rstd = lax.rsqrt(jnp.sum(sq, axis=-1, keepdims=True) * (1.0 / D_MODEL) + EPS)
    acts = []
    for c in range(_SLABS):
        lanes = slice(c * LANES, (c + 1) * LANES)
        z = (conv_buf[c] - mu) * rstd * lng_ref[:, lanes] + lnb_ref[:, lanes]
        acts.append((z * _sigmoid(z)).astype(_BF16))
    out = _dot(jnp.concatenate(acts, axis=-1), wout_ref[...]) + bout_ref[...]
    o_ref[...] = x + _rmsnorm(out, gpost_ref[...])


def _conformer_conv(x, g_pre, w_in, b_in, w_dw, b_dw, ln_g, ln_b, w_out, b_out, g_post):
    batch, seq_len, _ = x.shape
    tiles = seq_len // ROW_TILE
    halo_per_tile = ROW_TILE // CONV_HALO
    last_halo = seq_len // CONV_HALO - 1
    return pl.pallas_call(
        _conv_kernel,
        grid=(batch, tiles),
        in_specs=[
            pl.BlockSpec((None, CONV_HALO, D_MODEL), lambda b, i: (b, jnp.maximum(i * halo_per_tile - 1, 0), 0)),
            pl.BlockSpec((None, ROW_TILE, D_MODEL), lambda b, i: (b, i, 0)),
            pl.BlockSpec((None, CONV_HALO, D_MODEL), lambda b, i: (b, jnp.minimum((i + 1) * halo_per_tile, last_halo), 0)),
            _resident((1, D_MODEL)),
            _resident((D_MODEL, 2 * D_MODEL)),
            _resident((1, 2 * D_MODEL)),
            _resident((CONV_WIDTH, D_MODEL)),
            _resident((1, D_MODEL)),
            _resident((1, D_MODEL)),
            _resident((1, D_MODEL)),
            _resident((D_MODEL, D_MODEL)),
            _resident((1, D_MODEL)),
            _resident((1, D_MODEL)),
        ],
        out_specs=pl.BlockSpec((None, ROW_TILE, D_MODEL), lambda b, i: (b, i, 0)),
        out_shape=jax.ShapeDtypeStruct((batch, seq_len, D_MODEL), _F32),
        scratch_shapes=[
            pltpu.VMEM((_SLABS, ROW_TILE + 2 * CONV_HALO, LANES), _F32),
            pltpu.VMEM((_SLABS, ROW_TILE, LANES), _F32),
        ],
        compiler_params=_params("parallel", "parallel"),
        name="conformer_conv",
    )(x, x, x, g_pre, w_in, b_in, w_dw, b_dw, ln_g, ln_b, w_out, b_out, g_post)


def _row(v):
    return v.reshape(1, -1)


def _trunk(x, p):
    batch, seq_len, _ = x.shape
    rows = batch * seq_len
    for layer in range(DEPTH):
        li = layer // N_MIXERS
        if layer % N_MIXERS == 0:
            qt, k, vt = _qkv_proj(x, _row(p["g_mix_pre"][layer]), p["w_qt"][li], p["w_k"][li], p["w_vt"][li])
            x = _attention(qt, k, vt, p["alibi"], p["sink_rows"][li], x, p["w_o"][li], _row(p["g_mix_post"][layer]))
        else:
            x = _conformer_conv(x, _row(p["g_mix_pre"][layer]), p["conv_w_in"][li], _row(p["conv_b_in"][li]),
                                p["conv_w_dw"][li], _row(p["conv_b_dw"][li]), _row(p["conv_ln_g"][li]),
                                _row(p["conv_ln_b"][li]), p["conv_w_out"][li], _row(p["conv_b_out"][li]),
                                _row(p["g_mix_post"][layer]))
        x2d = _ffn(x.reshape(rows, D_MODEL), _row(p["g_ffn_pre"][layer]), p["ffn_w_gate"][layer],
                   p["ffn_w_up"][layer], p["ffn_w_down"][layer], _row(p["g_ffn_post"][layer]))
        x = x2d.reshape(batch, seq_len, D_MODEL)
    return x


def kernel(x_prompt, x_sample, g_mix_pre, g_mix_post, g_ffn_pre, g_ffn_post, w_qkv, w_o, attn_sink,
           conv_w_in, conv_b_in, conv_w_dw, conv_b_dw, conv_ln_g, conv_ln_b, conv_w_out, conv_b_out,
           ffn_w_gu, ffn_w_down):
    n_layers = attn_sink.shape[0]
    sink_rows = jnp.repeat((attn_sink * LOG2E).reshape(n_layers, N_KV_HEADS, 1, GROUP), BLOCK, axis=-1)

    params = dict(
        g_mix_pre=g_mix_pre, g_mix_post=g_mix_post, g_ffn_pre=g_ffn_pre, g_ffn_post=g_ffn_post,
        w_qt=jnp.swapaxes(w_qkv[:, :, :Q_COLS], 1, 2).astype(_BF16),
        w_k=w_qkv[:, :, Q_COLS:Q_COLS + KV_HEAD_COLS].astype(_BF16),
        w_vt=jnp.swapaxes(w_qkv[:, :, Q_COLS + KV_HEAD_COLS:], 1, 2).astype(_BF16),
        w_o=w_o.astype(_BF16), alibi=jnp.asarray(_alibi_table()), sink_rows=sink_rows,
        conv_w_in=conv_w_in.astype(_BF16), conv_b_in=conv_b_in, conv_w_dw=conv_w_dw, conv_b_dw=conv_b_dw,
        conv_ln_g=conv_ln_g, conv_ln_b=conv_ln_b, conv_w_out=conv_w_out.astype(_BF16), conv_b_out=conv_b_out,
        ffn_w_gate=ffn_w_gu[:, :, :D_FF].astype(_BF16), ffn_w_up=ffn_w_gu[:, :, D_FF:].astype(_BF16),
        ffn_w_down=ffn_w_down.astype(_BF16),
    )
    return (_trunk(x_prompt, params), _trunk(x_sample, params))
```

```python
import functools
import math

import jax
import jax.numpy as jnp
import numpy as np
from jax import lax
from jax.experimental import pallas as pl
from jax.experimental.pallas import tpu as pltpu

D_MODEL = 1024
DEPTH = 4
N_MIXERS = 2
N_HEADS = 16
N_KV_HEADS = 4
HEAD_DIM = D_MODEL // N_HEADS
GROUP = N_HEADS // N_KV_HEADS
WINDOW = 128
BLOCK = 128
SPAN = BLOCK + 2 * WINDOW
CONV_WIDTH = 31
CONV_PAD = CONV_WIDTH // 2
D_FF = -(-8 * D_MODEL // (3 * 256)) * 256
EPS = 1e-6
NEG_INF = -1e30
LOG2E = math.log2(math.e)

Q_COLS = N_HEADS * HEAD_DIM
KV_HEAD_COLS = N_KV_HEADS * HEAD_DIM
GROUP_COLS = GROUP * BLOCK
CONV_HALO = 16
LANES = 128

V7X_VMEM_BYTES = 64 * 1024 * 1024
VMEM_LIMIT_BYTES = 60000 * 1024
assert VMEM_LIMIT_BYTES < V7X_VMEM_BYTES

ROW_TILE = 1024
ATTN_TILE = 1024
FF_CHUNKS = (768, 768, 768, 512)
assert sum(FF_CHUNKS) == D_FF

_BF16 = jnp.bfloat16
_F32 = jnp.float32


def _params(*semantics):
    return pltpu.CompilerParams(dimension_semantics=semantics, vmem_limit_bytes=VMEM_LIMIT_BYTES)


def _resident(shape):
    zeros = (0,) * len(shape)
    return pl.BlockSpec(shape, lambda *_: zeros, pipeline_mode=pl.Buffered(1))


def _rmsnorm(x, g):
    return x * lax.rsqrt(jnp.mean(x * x, axis=-1, keepdims=True) + EPS) * g


def _sigmoid(x):
    return 1.0 / (1.0 + jnp.exp(-x))


def _dot(a, b):
    return jnp.dot(a, b, preferred_element_type=_F32)


def _dot_nt(a, b):
    return lax.dot_general(a, b, (((1,), (1,)), ((), ())), preferred_element_type=_F32)


def _dot_tn(a, b):
    return lax.dot_general(a, b, (((0,), (0,)), ((), ())), preferred_element_type=_F32)


def _qkv_kernel(x_ref, g_ref, wqt_ref, wk_ref, wvt_ref, qt_ref, k_ref, vt_ref):
    h = _rmsnorm(x_ref[...], g_ref[...]).astype(_BF16)
    qt = (_dot_nt(wqt_ref[...], h) * (HEAD_DIM ** -0.5 * LOG2E)).astype(_BF16)
    for j in range(ROW_TILE // BLOCK):
        qt_ref[j] = qt[:, j * BLOCK:(j + 1) * BLOCK]
    k_ref[...] = _dot(h, wk_ref[...]).astype(_BF16)
    vt_ref[...] = _dot_nt(wvt_ref[...], h).astype(_BF16)


def _qkv_proj(x, g, w_qt, w_k, w_vt):
    batch, seq_len, _ = x.shape
    return pl.pallas_call(
        _qkv_kernel,
        grid=(batch, seq_len // ROW_TILE),
        in_specs=[
            pl.BlockSpec((None, ROW_TILE, D_MODEL), lambda b, i: (b, i, 0)),
            _resident((1, D_MODEL)),
            _resident((Q_COLS, D_MODEL)),
            _resident((D_MODEL, KV_HEAD_COLS)),
            _resident((KV_HEAD_COLS, D_MODEL)),
        ],
        out_specs=[
            pl.BlockSpec((None, ROW_TILE // BLOCK, Q_COLS, BLOCK), lambda b, i: (b, i, 0, 0)),
            pl.BlockSpec((None, ROW_TILE, KV_HEAD_COLS), lambda b, i: (b, i, 0)),
            pl.BlockSpec((None, KV_HEAD_COLS, ROW_TILE), lambda b, i: (b, 0, i)),
        ],
        out_shape=[
            jax.ShapeDtypeStruct((batch, seq_len // BLOCK, Q_COLS, BLOCK), _BF16),
            jax.ShapeDtypeStruct((batch, seq_len, KV_HEAD_COLS), _BF16),
            jax.ShapeDtypeStruct((batch, KV_HEAD_COLS, seq_len), _BF16),
        ],
        compiler_params=_params("parallel", "parallel"),
        name="qkv_proj",
    )(x, g, w_qt, w_k, w_vt)


def _alibi_table():
    s = np.arange(SPAN, dtype=np.int64)[:, None]
    i = np.arange(BLOCK, dtype=np.int64)[None, :]
    dist = np.abs(i - s + WINDOW).astype(np.float32)
    heads = np.arange(1, N_HEADS + 1, dtype=np.float32)
    slopes = np.power(2.0, -8.0 * heads / N_HEADS).astype(np.float32)
    table = (-slopes[:, None, None] * dist[None]) * np.float32(LOG2E)
    table = table.reshape(N_KV_HEADS, GROUP, SPAN, BLOCK).transpose(0, 2, 1, 3)
    return np.ascontiguousarray(table.reshape(N_KV_HEADS, SPAN, GROUP_COLS)).astype(np.float32)


_Q_BLOCKS = ATTN_TILE // BLOCK


def _attn_kernel(qt_ref, kp_ref, km_ref, kn_ref, vtp_ref, vtm_ref, vtn_ref, alibi_ref, sink_ref,
                 x_ref, wo_ref, g_ref, o_ref, k_buf, vt_buf, st_buf, ot_buf):
    i = pl.program_id(1)
    last = pl.num_programs(1) - 1
    k_buf[0:WINDOW, :] = kp_ref[...]
    k_buf[WINDOW:WINDOW + ATTN_TILE, :] = km_ref[...]
    k_buf[WINDOW + ATTN_TILE:, :] = kn_ref[...]
    vt_buf[0] = vtp_ref[...]
    for j in range(_Q_BLOCKS):
        vt_buf[1 + j] = vtm_ref[:, j * BLOCK:(j + 1) * BLOCK]
    vt_buf[_Q_BLOCKS + 1] = vtn_ref[...]

    def fold8(v, op):
        return op(v.reshape(BLOCK // 8, 8, v.shape[-1]), axis=0)

    def scores(jb, kvh, slot):
        lo = kvh * HEAD_DIM
        h0 = kvh * GROUP * HEAD_DIM
        q_heads = jnp.concatenate(
            [qt_ref[jb, h0 + g * HEAD_DIM:h0 + (g + 1) * HEAD_DIM, :] for g in range(GROUP)], axis=1)
        pads = (lo, KV_HEAD_COLS - lo - HEAD_DIM)
        above, below = [[jnp.zeros((n, GROUP_COLS), _BF16)] if n else [] for n in pads]
        q_rhs = jnp.concatenate(above + [q_heads] + below, axis=0)
        r0 = pl.multiple_of(jb * BLOCK, BLOCK)
        st_buf[slot] = _dot(k_buf[pl.ds(r0, SPAN), :], q_rhs)

    def softmax_pv(jb, kvh, slot):
        key = lax.broadcasted_iota(jnp.int32, (BLOCK, GROUP_COLS), 0)
        qry = lax.broadcasted_iota(jnp.int32, (1, GROUP_COLS), 1) & (BLOCK - 1)
        first_key = jnp.where(jnp.logical_or(jb > 0, i > 0), qry, BLOCK)
        last_key = jnp.where(jnp.logical_or(jb < _Q_BLOCKS - 1, i < last), qry, -1)
        lo = kvh * HEAD_DIM
        bias = alibi_ref[kvh]
        sl = jnp.where(key >= first_key, st_buf[slot, 0:BLOCK] + bias[0:BLOCK], NEG_INF)
        sc = st_buf[slot, BLOCK:2 * BLOCK] + bias[BLOCK:2 * BLOCK]
        sr = jnp.where(key <= last_key, st_buf[slot, 2 * BLOCK:] + bias[2 * BLOCK:], NEG_INF)
        sink = sink_ref[kvh]
        m8 = jnp.maximum(jnp.maximum(fold8(sl, jnp.max), fold8(sc, jnp.max)), fold8(sr, jnp.max))
        m = jnp.maximum(jnp.max(m8, axis=0, keepdims=True), sink)
        pl_, pc, pr = jnp.exp2(sl - m), jnp.exp2(sc - m), jnp.exp2(sr - m)
        d8 = fold8(pl_, jnp.sum) + fold8(pc, jnp.sum) + fold8(pr, jnp.sum)
        denom = jnp.sum(d8, axis=0, keepdims=True) + jnp.exp2(sink - m)
        p = jnp.concatenate([pl_, pc, pr], axis=0).astype(_BF16)
        vt = jnp.concatenate([vt_buf[jb + t, lo:lo + HEAD_DIM, :] for t in range(SPAN // BLOCK)], axis=1)
        ot = _dot(vt, p) * (1.0 / denom)
        for g in range(GROUP):
            h0 = (kvh * GROUP + g) * HEAD_DIM
            ot_buf[jb, h0:h0 + HEAD_DIM, :] = ot[:, g * BLOCK:(g + 1) * BLOCK].astype(_BF16)

    scores(0, 0, 0)

    def block_step(jb, carry):
        for kvh in range(N_KV_HEADS):
            if kvh + 1 < N_KV_HEADS:
                scores(jb, kvh + 1, (kvh + 1) % 2)
            else:
                scores(jnp.minimum(jb + 1, _Q_BLOCKS - 1), 0, 0)
            softmax_pv(jb, kvh, kvh % 2)
        return carry

    lax.fori_loop(0, _Q_BLOCKS, block_step, 0)

    ot_all = jnp.concatenate([ot_buf[j] for j in range(_Q_BLOCKS)], axis=1)
    o_ref[...] = x_ref[...] + _rmsnorm(_dot_tn(ot_all, wo_ref[...]), g_ref[...])


def _attention(qt, k, vt, alibi, sink_rows, x, w_o, g_post):
    batch, seq_len, _ = x.shape
    halo_per_tile = ATTN_TILE // WINDOW
    last_halo = seq_len // WINDOW - 1

    def prev_blk(i):
        return jnp.maximum(i * halo_per_tile - 1, 0)

    def next_blk(i):
        return jnp.minimum((i + 1) * halo_per_tile, last_halo)

    return pl.pallas_call(
        _attn_kernel,
        grid=(batch, seq_len // ATTN_TILE),
        in_specs=[
            pl.BlockSpec((None, _Q_BLOCKS, Q_COLS, BLOCK), lambda b, i: (b, i, 0, 0)),
            pl.BlockSpec((None, WINDOW, KV_HEAD_COLS), lambda b, i: (b, prev_blk(i), 0)),
            pl.BlockSpec((None, ATTN_TILE, KV_HEAD_COLS), lambda b, i: (b, i, 0)),
            pl.BlockSpec((None, WINDOW, KV_HEAD_COLS), lambda b, i: (b, next_blk(i), 0)),
            pl.BlockSpec((None, KV_HEAD_COLS, WINDOW), lambda b, i: (b, 0, prev_blk(i))),
            pl.BlockSpec((None, KV_HEAD_COLS, ATTN_TILE), lambda b, i: (b, 0, i)),
            pl.BlockSpec((None, KV_HEAD_COLS, WINDOW), lambda b, i: (b, 0, next_blk(i))),
            _resident((N_KV_HEADS, SPAN, GROUP_COLS)),
            _resident((N_KV_HEADS, 1, GROUP_COLS)),
            pl.BlockSpec((None, ATTN_TILE, D_MODEL), lambda b, i: (b, i, 0)),
            _resident((Q_COLS, D_MODEL)),
            _resident((1, D_MODEL)),
        ],
        out_specs=pl.BlockSpec((None, ATTN_TILE, D_MODEL), lambda b, i: (b, i, 0)),
        out_shape=jax.ShapeDtypeStruct((batch, seq_len, D_MODEL), _F32),
        scratch_shapes=[
            pltpu.VMEM((ATTN_TILE + 2 * WINDOW, KV_HEAD_COLS), _BF16),
            pltpu.VMEM((_Q_BLOCKS + 2, KV_HEAD_COLS, BLOCK), _BF16),
            pltpu.VMEM((2, SPAN, GROUP_COLS), _F32),
            pltpu.VMEM((_Q_BLOCKS, Q_COLS, BLOCK), _BF16),
        ],
        compiler_params=_params("parallel", "parallel"),
        name="banded_gqa",
    )(qt, k, k, k, vt, vt, vt, alibi, sink_rows, x, w_o, g_post)


def _ffn_kernel(x_ref, gpre_ref, wg_ref, wu_ref, wd_ref, gpost_ref, o_ref):
    x = x_ref[...]
    h = _rmsnorm(x, gpre_ref[...]).astype(_BF16)
    acc = None
    c0 = 0
    for width in FF_CHUNKS:
        gate = _dot(h, wg_ref[:, c0:c0 + width])
        up = _dot(h, wu_ref[:, c0:c0 + width])
        act = (gate * _sigmoid(gate) * up).astype(_BF16)
        part = _dot(act, wd_ref[c0:c0 + width, :])
        acc = part if acc is None else acc + part
        c0 += width
    o_ref[...] = x + _rmsnorm(acc, gpost_ref[...])


def _ffn(x2d, g_pre, w_gate, w_up, w_down, g_post):
    rows = x2d.shape[0]
    return pl.pallas_call(
        _ffn_kernel,
        grid=(rows // ROW_TILE,),
        in_specs=[
            pl.BlockSpec((ROW_TILE, D_MODEL), lambda i: (i, 0)),
            _resident((1, D_MODEL)),
            _resident((D_MODEL, D_FF)),
            _resident((D_MODEL, D_FF)),
            _resident((D_FF, D_MODEL)),
            _resident((1, D_MODEL)),
        ],
        out_specs=pl.BlockSpec((ROW_TILE, D_MODEL), lambda i: (i, 0)),
        out_shape=jax.ShapeDtypeStruct((rows, D_MODEL), _F32),
        compiler_params=_params("parallel"),
        name="swiglu_ffn",
    )(x2d, g_pre, w_gate, w_up, w_down, g_post)


_SLABS = D_MODEL // LANES
_HALF = ROW_TILE // 2


def _conv_kernel(xp_ref, xm_ref, xn_ref, gpre_ref, win_ref, bin_ref, wdw_ref, bdw_ref,
                 lng_ref, lnb_ref, wout_ref, bout_ref, gpost_ref, o_ref, glu_buf, conv_buf):
    i = pl.program_id(1)
    last = pl.num_programs(1) - 1

    x = xm_ref[...]
    x_all = jnp.concatenate([xp_ref[...], x, xn_ref[...]], axis=0)
    h = _rmsnorm(x_all, gpre_ref[...]).astype(_BF16)
    y = _dot(h, win_ref[...]) + bin_ref[...]
    glu = y[:, :D_MODEL] * _sigmoid(y[:, D_MODEL:])
    for c in range(_SLABS):
        glu_buf[c] = glu[:, c * LANES:(c + 1) * LANES]

    @pl.when(i == 0)
    def _():
        glu_buf[:, 0:CONV_HALO, :] = jnp.zeros((_SLABS, CONV_HALO, LANES), _F32)

    @pl.when(i == last)
    def _():
        glu_buf[:, CONV_HALO + ROW_TILE:, :] = jnp.zeros((_SLABS, CONV_HALO, LANES), _F32)

    for c in range(_SLABS):
        lanes = slice(c * LANES, (c + 1) * LANES)
        for parity in range(2):
            acc = None
            for k in range(CONV_WIDTH):
                start = CONV_HALO - CONV_PAD + parity + k
                term = glu_buf[c, pl.ds(start, _HALF, stride=2), :] * wdw_ref[k:k + 1, lanes]
                acc = term if acc is None else acc + term
            conv_buf[c, pl.ds(parity, _HALF, stride=2), :] = acc + bdw_ref[:, lanes]

    total = conv_buf[0]
    for c in range(1, _SLABS):
        total = total + conv_buf[c]
    mu = jnp.sum(total, axis=-1, keepdims=True) * (1.0 / D_MODEL)
    sq = None
    for c in range(_SLABS):
        d = conv_buf[c] - mu
        sq = d * d if sq is None else sq + d * d
    rstd = lax.rsqrt(jnp.sum(sq, axis=-1, keepdims=True) * (1.0 / D_MODEL) + EPS)
    acts = []
    for c in range(_SLABS):
        lanes = slice(c * LANES, (c + 1) * LANES)
        z = (conv_buf[c] - mu) * rstd * lng_ref[:, lanes] + lnb_ref[:, lanes]
        acts.append((z * _sigmoid(z)).astype(_BF16))
    out = _dot(jnp.concatenate(acts, axis=-1), wout_ref[...]) + bout_ref[...]
    o_ref[...] = x + _rmsnorm(out, gpost_ref[...])


def _conformer_conv(x, g_pre, w_in, b_in, w_dw, b_dw, ln_g, ln_b, w_out, b_out, g_post):
    batch, seq_len, _ = x.shape
    tiles = seq_len // ROW_TILE
    halo_per_tile = ROW_TILE // CONV_HALO
    last_halo = seq_len // CONV_HALO - 1
    return pl.pallas_call(
        _conv_kernel,
        grid=(batch, tiles),
        in_specs=[
            pl.BlockSpec((None, CONV_HALO, D_MODEL), lambda b, i: (b, jnp.maximum(i * halo_per_tile - 1, 0), 0)),
            pl.BlockSpec((None, ROW_TILE, D_MODEL), lambda b, i: (b, i, 0)),
            pl.BlockSpec((None, CONV_HALO, D_MODEL), lambda b, i: (b, jnp.minimum((i + 1) * halo_per_tile, last_halo), 0)),
            _resident((1, D_MODEL)),
            _resident((D_MODEL, 2 * D_MODEL)),
            _resident((1, 2 * D_MODEL)),
            _resident((CONV_WIDTH, D_MODEL)),
            _resident((1, D_MODEL)),
            _resident((1, D_MODEL)),
            _resident((1, D_MODEL)),
            _resident((D_MODEL, D_MODEL)),
            _resident((1, D_MODEL)),
            _resident((1, D_MODEL)),
        ],
        out_specs=pl.BlockSpec((None, ROW_TILE, D_MODEL), lambda b, i: (b, i, 0)),
        out_shape=jax.ShapeDtypeStruct((batch, seq_len, D_MODEL), _F32),
        scratch_shapes=[
            pltpu.VMEM((_SLABS, ROW_TILE + 2 * CONV_HALO, LANES), _F32),
            pltpu.VMEM((_SLABS, ROW_TILE, LANES), _F32),
        ],
        compiler_params=_params("parallel", "parallel"),
        name="conformer_conv",
    )(x, x, x, g_pre, w_in, b_in, w_dw, b_dw, ln_g, ln_b, w_out, b_out, g_post)


def _row(v):
    return v.reshape(1, -1)


def _trunk(x, p):
    batch, seq_len, _ = x.shape
    rows = batch * seq_len
    for layer in range(DEPTH):
        li = layer // N_MIXERS
        if layer % N_MIXERS == 0:
            qt, k, vt = _qkv_proj(x, _row(p["g_mix_pre"][layer]), p["w_qt"][li], p["w_k"][li], p["w_vt"][li])
            x = _attention(qt, k, vt, p["alibi"], p["sink_rows"][li], x, p["w_o"][li], _row(p["g_mix_post"][layer]))
        else:
            x = _conformer_conv(x, _row(p["g_mix_pre"][layer]), p["conv_w_in"][li], _row(p["conv_b_in"][li]),
                                p["conv_w_dw"][li], _row(p["conv_b_dw"][li]), _row(p["conv_ln_g"][li]),
                                _row(p["conv_ln_b"][li]), p["conv_w_out"][li], _row(p["conv_b_out"][li]),
                                _row(p["g_mix_post"][layer]))
        x2d = _ffn(x.reshape(rows, D_MODEL), _row(p["g_ffn_pre"][layer]), p["ffn_w_gate"][layer],
                   p["ffn_w_up"][layer], p["ffn_w_down"][layer], _row(p["g_ffn_post"][layer]))
        x = x2d.reshape(batch, seq_len, D_MODEL)
    return x


def kernel(x_prompt, x_sample, g_mix_pre, g_mix_post, g_ffn_pre, g_ffn_post, w_qkv, w_o, attn_sink,
           conv_w_in, conv_b_in, conv_w_dw, conv_b_dw, conv_ln_g, conv_ln_b, conv_w_out, conv_b_out,
           ffn_w_gu, ffn_w_down):
    n_layers = attn_sink.shape[0]
    sink_rows = jnp.repeat((attn_sink * LOG2E).reshape(n_layers, N_KV_HEADS, 1, GROUP), BLOCK, axis=-1)

    params = dict(
        g_mix_pre=g_mix_pre, g_mix_post=g_mix_post, g_ffn_pre=g_ffn_pre, g_ffn_post=g_ffn_post,
        w_qt=jnp.swapaxes(w_qkv[:, :, :Q_COLS], 1, 2).astype(_BF16),
        w_k=w_qkv[:, :, Q_COLS:Q_COLS + KV_HEAD_COLS].astype(_BF16),
        w_vt=jnp.swapaxes(w_qkv[:, :, Q_COLS + KV_HEAD_COLS:], 1, 2).astype(_BF16),
        w_o=w_o.astype(_BF16), alibi=jnp.asarray(_alibi_table()), sink_rows=sink_rows,
        conv_w_in=conv_w_in.astype(_BF16), conv_b_in=conv_b_in, conv_w_dw=conv_w_dw, conv_b_dw=conv_b_dw,
        conv_ln_g=conv_ln_g, conv_ln_b=conv_ln_b, conv_w_out=conv_w_out.astype(_BF16), conv_b_out=conv_b_out,
        ffn_w_gate=ffn_w_gu[:, :, :D_FF].astype(_BF16), ffn_w_up=ffn_w_gu[:, :, D_FF:].astype(_BF16),
        ffn_w_down=ffn_w_down.astype(_BF16),
    )
    return (_trunk(x_prompt, params), _trunk(x_sample, params))
```

```python
import functools
import math

import jax
import jax.numpy as jnp
import numpy as np
from jax import lax
from jax.experimental import pallas as pl
from jax.experimental.pallas import tpu as pltpu

D_MODEL = 1024
DEPTH = 4
N_MIXERS = 2
N_HEADS = 16
N_KV_HEADS = 4
HEAD_DIM = D_MODEL // N_HEADS
GROUP = N_HEADS // N_KV_HEADS
WINDOW = 128
BLOCK = 128
SPAN = BLOCK + 2 * WINDOW
CONV_WIDTH = 31
CONV_PAD = CONV_WIDTH // 2
D_FF = -(-8 * D_MODEL // (3 * 256)) * 256
EPS = 1e-6
NEG_INF = -1e30
LOG2E = math.log2(math.e)

Q_COLS = N_HEADS * HEAD_DIM
KV_HEAD_COLS = N_KV_HEADS * HEAD_DIM
GROUP_COLS = GROUP * BLOCK
CONV_HALO = 16
LANES = 128

V7X_VMEM_BYTES = 64 * 1024 * 1024
VMEM_LIMIT_BYTES = 60000 * 1024
assert VMEM_LIMIT_BYTES < V7X_VMEM_BYTES

ROW_TILE = 1024
ATTN_TILE = 1024
FF_CHUNKS = (768, 768, 768, 512)
assert sum(FF_CHUNKS) == D_FF

_BF16 = jnp.bfloat16
_F32 = jnp.float32


def _params(*semantics):
    return pltpu.CompilerParams(dimension_semantics=semantics, vmem_limit_bytes=VMEM_LIMIT_BYTES)


def _resident(shape):
    zeros = (0,) * len(shape)
    return pl.BlockSpec(shape, lambda *_: zeros, pipeline_mode=pl.Buffered(1))


def _rmsnorm(x, g):
    return x * lax.rsqrt(jnp.mean(x * x, axis=-1, keepdims=True) + EPS) * g


def _sigmoid(x):
    return 1.0 / (1.0 + jnp.exp(-x))


def _dot(a, b):
    return jnp.dot(a, b, preferred_element_type=_F32)


def _dot_nt(a, b):
    return lax.dot_general(a, b, (((1,), (1,)), ((), ())), preferred_element_type=_F32)


def _dot_tn(a, b):
    return lax.dot_general(a, b, (((0,), (0,)), ((), ())), preferred_element_type=_F32)


def _qkv_kernel(x_ref, g_ref, wqt_ref, wk_ref, wvt_ref, qt_ref, k_ref, vt_ref):
    h = _rmsnorm(x_ref[...], g_ref[...]).astype(_BF16)
    qt = (_dot_nt(wqt_ref[...], h) * (HEAD_DIM ** -0.5 * LOG2E)).astype(_BF16)
    for j in range(ROW_TILE // BLOCK):
        qt_ref[j] = qt[:, j * BLOCK:(j + 1) * BLOCK]
    k_ref[...] = _dot(h, wk_ref[...]).astype(_BF16)
    vt_ref[...] = _dot_nt(wvt_ref[...], h).astype(_BF16)


def _qkv_proj(x, g, w_qt, w_k, w_vt):
    batch, seq_len, _ = x.shape
    return pl.pallas_call(
        _qkv_kernel,
        grid=(batch, seq_len // ROW_TILE),
        in_specs=[
            pl.BlockSpec((None, ROW_TILE, D_MODEL), lambda b, i: (b, i, 0)),
            _resident((1, D_MODEL)),
            _resident((Q_COLS, D_MODEL)),
            _resident((D_MODEL, KV_HEAD_COLS)),
            _resident((KV_HEAD_COLS, D_MODEL)),
        ],
        out_specs=[
            pl.BlockSpec((None, ROW_TILE // BLOCK, Q_COLS, BLOCK), lambda b, i: (b, i, 0, 0)),
            pl.BlockSpec((None, ROW_TILE, KV_HEAD_COLS), lambda b, i: (b, i, 0)),
            pl.BlockSpec((None, KV_HEAD_COLS, ROW_TILE), lambda b, i: (b, 0, i)),
        ],
        out_shape=[
            jax.ShapeDtypeStruct((batch, seq_len // BLOCK, Q_COLS, BLOCK), _BF16),
            jax.ShapeDtypeStruct((batch, seq_len, KV_HEAD_COLS), _BF16),
            jax.ShapeDtypeStruct((batch, KV_HEAD_COLS, seq_len), _BF16),
        ],
        compiler_params=_params("parallel", "parallel"),
        name="qkv_proj",
    )(x, g, w_qt, w_k, w_vt)


def _alibi_table():
    s = np.arange(SPAN, dtype=np.int64)[:, None]
    i = np.arange(BLOCK, dtype=np.int64)[None, :]
    dist = np.abs(i - s + WINDOW).astype(np.float32)
    heads = np.arange(1, N_HEADS + 1, dtype=np.float32)
    slopes = np.power(2.0, -8.0 * heads / N_HEADS).astype(np.float32)
    table = (-slopes[:, None, None] * dist[None]) * np.float32(LOG2E)
    table = table.reshape(N_KV_HEADS, GROUP, SPAN, BLOCK).transpose(0, 2, 1, 3)
    return np.ascontiguousarray(table.reshape(N_KV_HEADS, SPAN, GROUP_COLS)).astype(np.float32)


_Q_BLOCKS = ATTN_TILE // BLOCK


def _attn_kernel(qt_ref, kp_ref, km_ref, kn_ref, vtp_ref, vtm_ref, vtn_ref, alibi_ref, sink_ref,
                 x_ref, wo_ref, g_ref, o_ref, k_buf, vt_buf, st_buf, ot_buf):
    i = pl.program_id(1)
    last = pl.num_programs(1) - 1
    k_buf[0:WINDOW, :] = kp_ref[...]
    k_buf[WINDOW:WINDOW + ATTN_TILE, :] = km_ref[...]
    k_buf[WINDOW + ATTN_TILE:, :] = kn_ref[...]
    vt_buf[0] = vtp_ref[...]
    for j in range(_Q_BLOCKS):
        vt_buf[1 + j] = vtm_ref[:, j * BLOCK:(j + 1) * BLOCK]
    vt_buf[_Q_BLOCKS + 1] = vtn_ref[...]

    def fold8(v, op):
        return op(v.reshape(BLOCK // 8, 8, v.shape[-1]), axis=0)

    def scores(jb, kvh, slot):
        lo = kvh * HEAD_DIM
        h0 = kvh * GROUP * HEAD_DIM
        q_heads = jnp.concatenate(
            [qt_ref[jb, h0 + g * HEAD_DIM:h0 + (g + 1) * HEAD_DIM, :] for g in range(GROUP)], axis=1)
        pads = (lo, KV_HEAD_COLS - lo - HEAD_DIM)
        above, below = [[jnp.zeros((n, GROUP_COLS), _BF16)] if n else [] for n in pads]
        q_rhs = jnp.concatenate(above + [q_heads] + below, axis=0)
        r0 = pl.multiple_of(jb * BLOCK, BLOCK)
        st_buf[slot] = _dot(k_buf[pl.ds(r0, SPAN), :], q_rhs)

    def softmax_pv(jb, kvh, slot):
        key = lax.broadcasted_iota(jnp.int32, (BLOCK, GROUP_COLS), 0)
        qry = lax.broadcasted_iota(jnp.int32, (1, GROUP_COLS), 1) & (BLOCK - 1)
        first_key = jnp.where(jnp.logical_or(jb > 0, i > 0), qry, BLOCK)
        last_key = jnp.where(jnp.logical_or(jb < _Q_BLOCKS - 1, i < last), qry, -1)
        lo = kvh * HEAD_DIM
        bias = alibi_ref[kvh]
        sl = jnp.where(key >= first_key, st_buf[slot, 0:BLOCK] + bias[0:BLOCK], NEG_INF)
        sc = st_buf[slot, BLOCK:2 * BLOCK] + bias[BLOCK:2 * BLOCK]
        sr = jnp.where(key <= last_key, st_buf[slot, 2 * BLOCK:] + bias[2 * BLOCK:], NEG_INF)
        sink = sink_ref[kvh]
        m8 = jnp.maximum(jnp.maximum(fold8(sl, jnp.max), fold8(sc, jnp.max)), fold8(sr, jnp.max))
        m = jnp.maximum(jnp.max(m8, axis=0, keepdims=True), sink)
        pl_, pc, pr = jnp.exp2(sl - m), jnp.exp2(sc - m), jnp.exp2(sr - m)
        d8 = fold8(pl_, jnp.sum) + fold8(pc, jnp.sum) + fold8(pr, jnp.sum)
        denom = jnp.sum(d8, axis=0, keepdims=True) + jnp.exp2(sink - m)
        p = jnp.concatenate([pl_, pc, pr], axis=0).astype(_BF16)
        vt = jnp.concatenate([vt_buf[jb + t, lo:lo + HEAD_DIM, :] for t in range(SPAN // BLOCK)], axis=1)
        ot = _dot(vt, p) * (1.0 / denom)
        for g in range(GROUP):
            h0 = (kvh * GROUP + g) * HEAD_DIM
            ot_buf[jb, h0:h0 + HEAD_DIM, :] = ot[:, g * BLOCK:(g + 1) * BLOCK].astype(_BF16)

    scores(0, 0, 0)

    def block_step(jb, carry):
        for kvh in range(N_KV_HEADS):
            if kvh + 1 < N_KV_HEADS:
                scores(jb, kvh + 1, (kvh + 1) % 2)
            else:
                scores(jnp.minimum(jb + 1, _Q_BLOCKS - 1), 0, 0)
            softmax_pv(jb, kvh, kvh % 2)
        return carry

    lax.fori_loop(0, _Q_BLOCKS, block_step, 0)

    for half in range(2):
        blocks = range(half * (_Q_BLOCKS // 2), (half + 1) * (_Q_BLOCKS // 2))
        rows = slice(half * (ATTN_TILE // 2), (half + 1) * (ATTN_TILE // 2))
        ot_half = jnp.concatenate([ot_buf[j] for j in blocks], axis=1)
        o_ref[rows, :] = x_ref[rows, :] + _rmsnorm(_dot_tn(ot_half, wo_ref[...]), g_ref[...])


def _attention(qt, k, vt, alibi, sink_rows, x, w_o, g_post):
    batch, seq_len, _ = x.shape
    halo_per_tile = ATTN_TILE // WINDOW
    last_halo = seq_len // WINDOW - 1

    def prev_blk(i):
        return jnp.maximum(i * halo_per_tile - 1, 0)

    def next_blk(i):
        return jnp.minimum((i + 1) * halo_per_tile, last_halo)

    return pl.pallas_call(
        _attn_kernel,
        grid=(batch, seq_len // ATTN_TILE),
        in_specs=[
            pl.BlockSpec((None, _Q_BLOCKS, Q_COLS, BLOCK), lambda b, i: (b, i, 0, 0)),
            pl.BlockSpec((None, WINDOW, KV_HEAD_COLS), lambda b, i: (b, prev_blk(i), 0)),
            pl.BlockSpec((None, ATTN_TILE, KV_HEAD_COLS), lambda b, i: (b, i, 0)),
            pl.BlockSpec((None, WINDOW, KV_HEAD_COLS), lambda b, i: (b, next_blk(i), 0)),
            pl.BlockSpec((None, KV_HEAD_COLS, WINDOW), lambda b, i: (b, 0, prev_blk(i))),
            pl.BlockSpec((None, KV_HEAD_COLS, ATTN_TILE), lambda b, i: (b, 0, i)),
            pl.BlockSpec((None, KV_HEAD_COLS, WINDOW), lambda b, i: (b, 0, next_blk(i))),
            _resident((N_KV_HEADS, SPAN, GROUP_COLS)),
            _resident((N_KV_HEADS, 1, GROUP_COLS)),
            pl.BlockSpec((None, ATTN_TILE, D_MODEL), lambda b, i: (b, i, 0)),
            _resident((Q_COLS, D_MODEL)),
            _resident((1, D_MODEL)),
        ],
        out_specs=pl.BlockSpec((None, ATTN_TILE, D_MODEL), lambda b, i: (b, i, 0)),
        out_shape=jax.ShapeDtypeStruct((batch, seq_len, D_MODEL), _F32),
        scratch_shapes=[
            pltpu.VMEM((ATTN_TILE + 2 * WINDOW, KV_HEAD_COLS), _BF16),
            pltpu.VMEM((_Q_BLOCKS + 2, KV_HEAD_COLS, BLOCK), _BF16),
            pltpu.VMEM((2, SPAN, GROUP_COLS), _F32),
            pltpu.VMEM((_Q_BLOCKS, Q_COLS, BLOCK), _BF16),
        ],
        compiler_params=_params("parallel", "parallel"),
        name="banded_gqa",
    )(qt, k, k, k, vt, vt, vt, alibi, sink_rows, x, w_o, g_post)


_FFN_HALF = ROW_TILE // 2


def _ffn_kernel(x_ref, gpre_ref, wg_ref, wu_ref, wd_ref, gpost_ref, o_ref):
    def swiglu(h, between=None):
        acc = None
        c0 = 0
        for k, width in enumerate(FF_CHUNKS):
            gate = _dot(h, wg_ref[:, c0:c0 + width])
            up = _dot(h, wu_ref[:, c0:c0 + width])
            if k == 0 and between is not None:
                between()
            act = (gate * _sigmoid(gate) * up).astype(_BF16)
            part = _dot(act, wd_ref[c0:c0 + width, :])
            acc = part if acc is None else acc + part
            c0 += width
        return acc

    top, bottom = slice(0, _FFN_HALF), slice(_FFN_HALF, ROW_TILE)
    pre = {}

    def norm_bottom():
        pre["h"] = _rmsnorm(x_ref[bottom, :], gpre_ref[...]).astype(_BF16)

    acc_top = swiglu(_rmsnorm(x_ref[top, :], gpre_ref[...]).astype(_BF16), norm_bottom)

    def finish_top():
        o_ref[top, :] = x_ref[top, :] + _rmsnorm(acc_top, gpost_ref[...])

    acc_bottom = swiglu(pre["h"], finish_top)
    o_ref[bottom, :] = x_ref[bottom, :] + _rmsnorm(acc_bottom, gpost_ref[...])


def _ffn(x2d, g_pre, w_gate, w_up, w_down, g_post):
    rows = x2d.shape[0]
    return pl.pallas_call(
        _ffn_kernel,
        grid=(rows // ROW_TILE,),
        in_specs=[
            pl.BlockSpec((ROW_TILE, D_MODEL), lambda i: (i, 0)),
            _resident((1, D_MODEL)),
            _resident((D_MODEL, D_FF)),
            _resident((D_MODEL, D_FF)),
            _resident((D_FF, D_MODEL)),
            _resident((1, D_MODEL)),
        ],
        out_specs=pl.BlockSpec((ROW_TILE, D_MODEL), lambda i: (i, 0)),
        out_shape=jax.ShapeDtypeStruct((rows, D_MODEL), _F32),
        compiler_params=_params("parallel"),
        name="swiglu_ffn",
    )(x2d, g_pre, w_gate, w_up, w_down, g_post)


_SLABS = D_MODEL // LANES
_HALF = ROW_TILE // 2


def _conv_kernel(xp_ref, xm_ref, xn_ref, gpre_ref, win_ref, bin_ref, wdw_ref, bdw_ref,
                 lng_ref, lnb_ref, wout_ref, bout_ref, gpost_ref, o_ref, glu_buf, conv_buf):
    i = pl.program_id(1)
    last = pl.num_programs(1) - 1

    x = xm_ref[...]
    x_all = jnp.concatenate([xp_ref[...], x, xn_ref[...]], axis=0)
    h = _rmsnorm(x_all, gpre_ref[...]).astype(_BF16)
    y = _dot(h, win_ref[...]) + bin_ref[...]
    glu = y[:, :D_MODEL] * _sigmoid(y[:, D_MODEL:])
    for c in range(_SLABS):
        glu_buf[c] = glu[:, c * LANES:(c + 1) * LANES]

    @pl.when(i == 0)
    def _():
        glu_buf[:, 0:CONV_HALO, :] = jnp.zeros((_SLABS, CONV_HALO, LANES), _F32)

    @pl.when(i == last)
    def _():
        glu_buf[:, CONV_HALO + ROW_TILE:, :] = jnp.zeros((_SLABS, CONV_HALO, LANES), _F32)

    for c in range(_SLABS):
        lanes = slice(c * LANES, (c + 1) * LANES)
        for parity in range(2):
            acc = None
            for k in range(CONV_WIDTH):
                start = CONV_HALO - CONV_PAD + parity + k
                term = glu_buf[c, pl.ds(start, _HALF, stride=2), :] * wdw_ref[k:k + 1, lanes]
                acc = term if acc is None else acc + term
            conv_buf[c, pl.ds(parity, _HALF, stride=2), :] = acc + bdw_ref[:, lanes]

    total = conv_buf[0]
    for c in range(1, _SLABS):
        total = total + conv_buf[c]
    mu = jnp.sum(total, axis=-1, keepdims=True) * (1.0 / D_MODEL)
    sq = None
    for c in range(_SLABS):
        d = conv_buf[c] - mu
        sq = d * d if sq is None else sq + d * d
    rstd = lax.rsqrt(jnp.sum(sq, axis=-1, keepdims=True) * (1.0 / D_MODEL) + EPS)
    acts = []
    for c in range(_SLABS):
        lanes = slice(c * LANES, (c + 1) * LANES)
        z = (conv_buf[c] - mu) * rstd * lng_ref[:, lanes] + lnb_ref[:, lanes]
        acts.append((z * _sigmoid(z)).astype(_BF16))
    out = _dot(jnp.concatenate(acts, axis=-1), wout_ref[...]) + bout_ref[...]
    o_ref[...] = x + _rmsnorm(out, gpost_ref[...])


def _conformer_conv(x, g_pre, w_in, b_in, w_dw, b_dw, ln_g, ln_b, w_out, b_out, g_post):
    batch, seq_len, _ = x.shape
    tiles = seq_len // ROW_TILE
    halo_per_tile = ROW_TILE // CONV_HALO
    last_halo = seq_len // CONV_HALO - 1
    return pl.pallas_call(
        _conv_kernel,
        grid=(batch, tiles),
        in_specs=[
            pl.BlockSpec((None, CONV_HALO, D_MODEL), lambda b, i: (b, jnp.maximum(i * halo_per_tile - 1, 0), 0)),
            pl.BlockSpec((None, ROW_TILE, D_MODEL), lambda b, i: (b, i, 0)),
            pl.BlockSpec((None, CONV_HALO, D_MODEL), lambda b, i: (b, jnp.minimum((i + 1) * halo_per_tile, last_halo), 0)),
            _resident((1, D_MODEL)),
            _resident((D_MODEL, 2 * D_MODEL)),
            _resident((1, 2 * D_MODEL)),
            _resident((CONV_WIDTH, D_MODEL)),
            _resident((1, D_MODEL)),
            _resident((1, D_MODEL)),
            _resident((1, D_MODEL)),
            _resident((D_MODEL, D_MODEL)),
            _resident((1, D_MODEL)),
            _resident((1, D_MODEL)),
        ],
        out_specs=pl.BlockSpec((None, ROW_TILE, D_MODEL), lambda b, i: (b, i, 0)),
        out_shape=jax.ShapeDtypeStruct((batch, seq_len, D_MODEL), _F32),
        scratch_shapes=[
            pltpu.VMEM((_SLABS, ROW_TILE + 2 * CONV_HALO, LANES), _F32),
            pltpu.VMEM((_SLABS, ROW_TILE, LANES), _F32),
        ],
        compiler_params=_params("parallel", "parallel"),
        name="conformer_conv",
    )(x, x, x, g_pre, w_in, b_in, w_dw, b_dw, ln_g, ln_b, w_out, b_out, g_post)


def _row(v):
    return v.reshape(1, -1)


def _trunk(x, p):
    batch, seq_len, _ = x.shape
    rows = batch * seq_len
    for layer in range(DEPTH):
        li = layer // N_MIXERS
        if layer % N_MIXERS == 0:
            qt, k, vt = _qkv_proj(x, _row(p["g_mix_pre"][layer]), p["w_qt"][li], p["w_k"][li], p["w_vt"][li])
            x = _attention(qt, k, vt, p["alibi"], p["sink_rows"][li], x, p["w_o"][li], _row(p["g_mix_post"][layer]))
        else:
            x = _conformer_conv(x, _row(p["g_mix_pre"][layer]), p["conv_w_in"][li], _row(p["conv_b_in"][li]),
                                p["conv_w_dw"][li], _row(p["conv_b_dw"][li]), _row(p["conv_ln_g"][li]),
                                _row(p["conv_ln_b"][li]), p["conv_w_out"][li], _row(p["conv_b_out"][li]),
                                _row(p["g_mix_post"][layer]))
        x2d = _ffn(x.reshape(rows, D_MODEL), _row(p["g_ffn_pre"][layer]), p["ffn_w_gate"][layer],
                   p["ffn_w_up"][layer], p["ffn_w_down"][layer], _row(p["g_ffn_post"][layer]))
        x = x2d.reshape(batch, seq_len, D_MODEL)
    return x


def kernel(x_prompt, x_sample, g_mix_pre, g_mix_post, g_ffn_pre, g_ffn_post, w_qkv, w_o, attn_sink,
           conv_w_in, conv_b_in, conv_w_dw, conv_b_dw, conv_ln_g, conv_ln_b, conv_w_out, conv_b_out,
           ffn_w_gu, ffn_w_down):
    n_layers = attn_sink.shape[0]
    sink_rows = jnp.repeat((attn_sink * LOG2E).reshape(n_layers, N_KV_HEADS, 1, GROUP), BLOCK, axis=-1)

    params = dict(
        g_mix_pre=g_mix_pre, g_mix_post=g_mix_post, g_ffn_pre=g_ffn_pre, g_ffn_post=g_ffn_post,
        w_qt=jnp.swapaxes(w_qkv[:, :, :Q_COLS], 1, 2).astype(_BF16),
        w_k=w_qkv[:, :, Q_COLS:Q_COLS + KV_HEAD_COLS].astype(_BF16),
        w_vt=jnp.swapaxes(w_qkv[:, :, Q_COLS + KV_HEAD_COLS:], 1, 2).astype(_BF16),
        w_o=w_o.astype(_BF16), alibi=jnp.asarray(_alibi_table()), sink_rows=sink_rows,
        conv_w_in=conv_w_in.astype(_BF16), conv_b_in=conv_b_in, conv_w_dw=conv_w_dw, conv_b_dw=conv_b_dw,
        conv_ln_g=conv_ln_g, conv_ln_b=conv_ln_b, conv_w_out=conv_w_out.astype(_BF16), conv_b_out=conv_b_out,
        ffn_w_gate=ffn_w_gu[:, :, :D_FF].astype(_BF16), ffn_w_up=ffn_w_gu[:, :, D_FF:].astype(_BF16),
        ffn_w_down=ffn_w_down.astype(_BF16),
    )
    return (_trunk(x_prompt, params), _trunk(x_sample, params))
```

```python
import functools
import math

import jax
import jax.numpy as jnp
import numpy as np
from jax import lax
from jax.experimental import pallas as pl
from jax.experimental.pallas import tpu as pltpu

D_MODEL = 1024
DEPTH = 4
N_MIXERS = 2
N_HEADS = 16
N_KV_HEADS = 4
HEAD_DIM = D_MODEL // N_HEADS
GROUP = N_HEADS // N_KV_HEADS
WINDOW = 128
BLOCK = 128
SPAN = BLOCK + 2 * WINDOW
CONV_WIDTH = 31
CONV_PAD = CONV_WIDTH // 2
D_FF = -(-8 * D_MODEL // (3 * 256)) * 256
EPS = 1e-6
NEG_INF = -1e30
LOG2E = math.log2(math.e)

Q_COLS = N_HEADS * HEAD_DIM
KV_HEAD_COLS = N_KV_HEADS * HEAD_DIM
GROUP_COLS = GROUP * BLOCK
CONV_HALO = 16
LANES = 128

V7X_VMEM_BYTES = 64 * 1024 * 1024
VMEM_LIMIT_BYTES = 60000 * 1024
assert VMEM_LIMIT_BYTES < V7X_VMEM_BYTES

ROW_TILE = 1024
ATTN_TILE = 1024
FF_CHUNKS = (768, 768, 768, 512)
assert sum(FF_CHUNKS) == D_FF

_BF16 = jnp.bfloat16
_F32 = jnp.float32


def _params(*semantics):
    return pltpu.CompilerParams(dimension_semantics=semantics, vmem_limit_bytes=VMEM_LIMIT_BYTES)


def _resident(shape):
    zeros = (0,) * len(shape)
    return pl.BlockSpec(shape, lambda *_: zeros, pipeline_mode=pl.Buffered(1))


def _rmsnorm(x, g):
    return x * lax.rsqrt(jnp.mean(x * x, axis=-1, keepdims=True) + EPS) * g


def _sigmoid(x):
    return 1.0 / (1.0 + jnp.exp(-x))


def _dot(a, b):
    return jnp.dot(a, b, preferred_element_type=_F32)


def _dot_nt(a, b):
    return lax.dot_general(a, b, (((1,), (1,)), ((), ())), preferred_element_type=_F32)


def _dot_tn(a, b):
    return lax.dot_general(a, b, (((0,), (0,)), ((), ())), preferred_element_type=_F32)


def _qkv_kernel(x_ref, g_ref, wqt_ref, wk_ref, wvt_ref, qt_ref, k_ref, vt_ref):
    h = _rmsnorm(x_ref[...], g_ref[...]).astype(_BF16)
    qt = (_dot_nt(wqt_ref[...], h) * (HEAD_DIM ** -0.5 * LOG2E)).astype(_BF16)
    for j in range(ROW_TILE // BLOCK):
        qt_ref[j] = qt[:, j * BLOCK:(j + 1) * BLOCK]
    k_ref[...] = _dot(h, wk_ref[...]).astype(_BF16)
    vt_ref[...] = _dot_nt(wvt_ref[...], h).astype(_BF16)


def _qkv_proj(x, g, w_qt, w_k, w_vt):
    batch, seq_len, _ = x.shape
    return pl.pallas_call(
        _qkv_kernel,
        grid=(batch, seq_len // ROW_TILE),
        in_specs=[
            pl.BlockSpec((None, ROW_TILE, D_MODEL), lambda b, i: (b, i, 0)),
            _resident((1, D_MODEL)),
            _resident((Q_COLS, D_MODEL)),
            _resident((D_MODEL, KV_HEAD_COLS)),
            _resident((KV_HEAD_COLS, D_MODEL)),
        ],
        out_specs=[
            pl.BlockSpec((None, ROW_TILE // BLOCK, Q_COLS, BLOCK), lambda b, i: (b, i, 0, 0)),
            pl.BlockSpec((None, ROW_TILE, KV_HEAD_COLS), lambda b, i: (b, i, 0)),
            pl.BlockSpec((None, KV_HEAD_COLS, ROW_TILE), lambda b, i: (b, 0, i)),
        ],
        out_shape=[
            jax.ShapeDtypeStruct((batch, seq_len // BLOCK, Q_COLS, BLOCK), _BF16),
            jax.ShapeDtypeStruct((batch, seq_len, KV_HEAD_COLS), _BF16),
            jax.ShapeDtypeStruct((batch, KV_HEAD_COLS, seq_len), _BF16),
        ],
        compiler_params=_params("parallel", "parallel"),
        name="qkv_proj",
    )(x, g, w_qt, w_k, w_vt)


def _alibi_table():
    s = np.arange(SPAN, dtype=np.int64)[:, None]
    i = np.arange(BLOCK, dtype=np.int64)[None, :]
    dist = np.abs(i - s + WINDOW).astype(np.float32)
    heads = np.arange(1, N_HEADS + 1, dtype=np.float32)
    slopes = np.power(2.0, -8.0 * heads / N_HEADS).astype(np.float32)
    table = (-slopes[:, None, None] * dist[None]) * np.float32(LOG2E)
    table = table.reshape(N_KV_HEADS, GROUP, SPAN, BLOCK).transpose(0, 2, 1, 3)
    return np.ascontiguousarray(table.reshape(N_KV_HEADS, SPAN, GROUP_COLS)).astype(np.float32)


_Q_BLOCKS = ATTN_TILE // BLOCK


def _attn_kernel(qt_ref, kp_ref, km_ref, kn_ref, vtp_ref, vtm_ref, vtn_ref, alibi_ref, sink_ref,
                 x_ref, wo_ref, g_ref, o_ref, k_buf, vt_buf, st_buf, ot_buf):
    i = pl.program_id(1)
    last = pl.num_programs(1) - 1
    k_buf[0:WINDOW, :] = kp_ref[...]
    k_buf[WINDOW:WINDOW + ATTN_TILE, :] = km_ref[...]
    k_buf[WINDOW + ATTN_TILE:, :] = kn_ref[...]
    vt_buf[0] = vtp_ref[...]
    for j in range(_Q_BLOCKS):
        vt_buf[1 + j] = vtm_ref[:, j * BLOCK:(j + 1) * BLOCK]
    vt_buf[_Q_BLOCKS + 1] = vtn_ref[...]

    def fold8(v, op):
        return op(v.reshape(BLOCK // 8, 8, v.shape[-1]), axis=0)

    def scores(jb, kvh, slot):
        lo = kvh * HEAD_DIM
        h0 = kvh * GROUP * HEAD_DIM
        q_heads = jnp.concatenate(
            [qt_ref[jb, h0 + g * HEAD_DIM:h0 + (g + 1) * HEAD_DIM, :] for g in range(GROUP)], axis=1)
        pads = (lo, KV_HEAD_COLS - lo - HEAD_DIM)
        above, below = [[jnp.zeros((n, GROUP_COLS), _BF16)] if n else [] for n in pads]
        q_rhs = jnp.concatenate(above + [q_heads] + below, axis=0)
        r0 = pl.multiple_of(jb * BLOCK, BLOCK)
        st_buf[slot] = _dot(k_buf[pl.ds(r0, SPAN), :], q_rhs)

    def softmax_pv(jb, kvh, slot):
        key = lax.broadcasted_iota(jnp.int32, (BLOCK, GROUP_COLS), 0)
        qry = lax.broadcasted_iota(jnp.int32, (1, GROUP_COLS), 1) & (BLOCK - 1)
        first_key = jnp.where(jnp.logical_or(jb > 0, i > 0), qry, BLOCK)
        last_key = jnp.where(jnp.logical_or(jb < _Q_BLOCKS - 1, i < last), qry, -1)
        lo = kvh * HEAD_DIM
        bias = alibi_ref[kvh]
        sl = jnp.where(key >= first_key, st_buf[slot, 0:BLOCK] + bias[0:BLOCK], NEG_INF)
        sc = st_buf[slot, BLOCK:2 * BLOCK] + bias[BLOCK:2 * BLOCK]
        sr = jnp.where(key <= last_key, st_buf[slot, 2 * BLOCK:] + bias[2 * BLOCK:], NEG_INF)
        sink = sink_ref[kvh]
        m8 = jnp.maximum(jnp.maximum(fold8(sl, jnp.max), fold8(sc, jnp.max)), fold8(sr, jnp.max))
        m = jnp.maximum(jnp.max(m8, axis=0, keepdims=True), sink)
        pl_, pc, pr = jnp.exp2(sl - m), jnp.exp2(sc - m), jnp.exp2(sr - m)
        d8 = fold8(pl_, jnp.sum) + fold8(pc, jnp.sum) + fold8(pr, jnp.sum)
        denom = jnp.sum(d8, axis=0, keepdims=True) + jnp.exp2(sink - m)
        p = jnp.concatenate([pl_, pc, pr], axis=0).astype(_BF16)
        vt = jnp.concatenate([vt_buf[jb + t, lo:lo + HEAD_DIM, :] for t in range(SPAN // BLOCK)], axis=1)
        ot = _dot(vt, p) * (1.0 / denom)
        for g in range(GROUP):
            h0 = (kvh * GROUP + g) * HEAD_DIM
            ot_buf[jb, h0:h0 + HEAD_DIM, :] = ot[:, g * BLOCK:(g + 1) * BLOCK].astype(_BF16)

    scores(0, 0, 0)

    def block_step(jb, carry):
        for kvh in range(N_KV_HEADS):
            if kvh + 1 < N_KV_HEADS:
                scores(jb, kvh + 1, (kvh + 1) % 2)
            else:
                scores(jnp.minimum(jb + 1, _Q_BLOCKS - 1), 0, 0)
            softmax_pv(jb, kvh, kvh % 2)
        return carry

    lax.fori_loop(0, _Q_BLOCKS, block_step, 0)

    for half in range(2):
        blocks = range(half * (_Q_BLOCKS // 2), (half + 1) * (_Q_BLOCKS // 2))
        rows = slice(half * (ATTN_TILE // 2), (half + 1) * (ATTN_TILE // 2))
        ot_half = jnp.concatenate([ot_buf[j] for j in blocks], axis=1)
        o_ref[rows, :] = x_ref[rows, :] + _rmsnorm(_dot_tn(ot_half, wo_ref[...]), g_ref[...])


def _attention(qt, k, vt, alibi, sink_rows, x, w_o, g_post):
    batch, seq_len, _ = x.shape
    halo_per_tile = ATTN_TILE // WINDOW
    last_halo = seq_len // WINDOW - 1

    def prev_blk(i):
        return jnp.maximum(i * halo_per_tile - 1, 0)

    def next_blk(i):
        return jnp.minimum((i + 1) * halo_per_tile, last_halo)

    return pl.pallas_call(
        _attn_kernel,
        grid=(batch, seq_len // ATTN_TILE),
        in_specs=[
            pl.BlockSpec((None, _Q_BLOCKS, Q_COLS, BLOCK), lambda b, i: (b, i, 0, 0)),
            pl.BlockSpec((None, WINDOW, KV_HEAD_COLS), lambda b, i: (b, prev_blk(i), 0)),
            pl.BlockSpec((None, ATTN_TILE, KV_HEAD_COLS), lambda b, i: (b, i, 0)),
            pl.BlockSpec((None, WINDOW, KV_HEAD_COLS), lambda b, i: (b, next_blk(i), 0)),
            pl.BlockSpec((None, KV_HEAD_COLS, WINDOW), lambda b, i: (b, 0, prev_blk(i))),
            pl.BlockSpec((None, KV_HEAD_COLS, ATTN_TILE), lambda b, i: (b, 0, i)),
            pl.BlockSpec((None, KV_HEAD_COLS, WINDOW), lambda b, i: (b, 0, next_blk(i))),
            _resident((N_KV_HEADS, SPAN, GROUP_COLS)),
            _resident((N_KV_HEADS, 1, GROUP_COLS)),
            pl.BlockSpec((None, ATTN_TILE, D_MODEL), lambda b, i: (b, i, 0)),
            _resident((Q_COLS, D_MODEL)),
            _resident((1, D_MODEL)),
        ],
        out_specs=pl.BlockSpec((None, ATTN_TILE, D_MODEL), lambda b, i: (b, i, 0)),
        out_shape=jax.ShapeDtypeStruct((batch, seq_len, D_MODEL), _F32),
        scratch_shapes=[
            pltpu.VMEM((ATTN_TILE + 2 * WINDOW, KV_HEAD_COLS), _BF16),
            pltpu.VMEM((_Q_BLOCKS + 2, KV_HEAD_COLS, BLOCK), _BF16),
            pltpu.VMEM((2, SPAN, GROUP_COLS), _F32),
            pltpu.VMEM((_Q_BLOCKS, Q_COLS, BLOCK), _BF16),
        ],
        compiler_params=_params("parallel", "parallel"),
        name="banded_gqa",
    )(qt, k, k, k, vt, vt, vt, alibi, sink_rows, x, w_o, g_post)


_FFN_HALF = ROW_TILE // 2


def _ffn_kernel(x_ref, gpre_ref, wgu_ref, wd_ref, gpost_ref, o_ref):
    def swiglu(h, between=None):
        acc = None
        c0 = 0
        for k, width in enumerate(FF_CHUNKS):
            gate = _dot(h, wgu_ref[:, c0:c0 + width])
            up = _dot(h, wgu_ref[:, D_FF + c0:D_FF + c0 + width])
            if k == 0 and between is not None:
                between()
            act = (gate * _sigmoid(gate) * up).astype(_BF16)
            part = _dot(act, wd_ref[c0:c0 + width, :])
            acc = part if acc is None else acc + part
            c0 += width
        return acc

    top, bottom = slice(0, _FFN_HALF), slice(_FFN_HALF, ROW_TILE)
    pre = {}

    def norm_bottom():
        pre["h"] = _rmsnorm(x_ref[bottom, :], gpre_ref[...]).astype(_BF16)

    acc_top = swiglu(_rmsnorm(x_ref[top, :], gpre_ref[...]).astype(_BF16), norm_bottom)

    def finish_top():
        o_ref[top, :] = x_ref[top, :] + _rmsnorm(acc_top, gpost_ref[...])

    acc_bottom = swiglu(pre["h"], finish_top)
    o_ref[bottom, :] = x_ref[bottom, :] + _rmsnorm(acc_bottom, gpost_ref[...])


def _ffn(x2d, g_pre, w_gu, w_down, g_post):
    rows = x2d.shape[0]
    return pl.pallas_call(
        _ffn_kernel,
        grid=(rows // ROW_TILE,),
        in_specs=[
            pl.BlockSpec((ROW_TILE, D_MODEL), lambda i: (i, 0)),
            _resident((1, D_MODEL)),
            _resident((D_MODEL, 2 * D_FF)),
            _resident((D_FF, D_MODEL)),
            _resident((1, D_MODEL)),
        ],
        out_specs=pl.BlockSpec((ROW_TILE, D_MODEL), lambda i: (i, 0)),
        out_shape=jax.ShapeDtypeStruct((rows, D_MODEL), _F32),
        compiler_params=_params("parallel"),
        name="swiglu_ffn",
    )(x2d, g_pre, w_gu, w_down, g_post)


_SLABS = D_MODEL // LANES
_HALF = ROW_TILE // 2


def _conv_kernel(xp_ref, xm_ref, xn_ref, gpre_ref, win_ref, bin_ref, wdw_ref, bdw_ref,
                 lng_ref, lnb_ref, wout_ref, bout_ref, gpost_ref, o_ref, glu_buf, conv_buf):
    i = pl.program_id(1)
    last = pl.num_programs(1) - 1

    x = xm_ref[...]
    x_all = jnp.concatenate([xp_ref[...], x, xn_ref[...]], axis=0)
    h = _rmsnorm(x_all, gpre_ref[...]).astype(_BF16)
    y = _dot(h, win_ref[...]) + bin_ref[...]
    glu = y[:, :D_MODEL] * _sigmoid(y[:, D_MODEL:])
    for c in range(_SLABS):
        glu_buf[c] = glu[:, c * LANES:(c + 1) * LANES]

    @pl.when(i == 0)
    def _():
        glu_buf[:, 0:CONV_HALO, :] = jnp.zeros((_SLABS, CONV_HALO, LANES), _F32)

    @pl.when(i == last)
    def _():
        glu_buf[:, CONV_HALO + ROW_TILE:, :] = jnp.zeros((_SLABS, CONV_HALO, LANES), _F32)

    for c in range(_SLABS):
        lanes = slice(c * LANES, (c + 1) * LANES)
        for parity in range(2):
            acc = None
            for k in range(CONV_WIDTH):
                start = CONV_HALO - CONV_PAD + parity + k
                term = glu_buf[c, pl.ds(start, _HALF, stride=2), :] * wdw_ref[k:k + 1, lanes]
                acc = term if acc is None else acc + term
            conv_buf[c, pl.ds(parity, _HALF, stride=2), :] = acc + bdw_ref[:, lanes]

    total = conv_buf[0]
    for c in range(1, _SLABS):
        total = total + conv_buf[c]
    mu = jnp.sum(total, axis=-1, keepdims=True) * (1.0 / D_MODEL)
    sq = None
    for c in range(_SLABS):
        d = conv_buf[c] - mu
        sq = d * d if sq is None else sq + d * d
    rstd = lax.rsqrt(jnp.sum(sq, axis=-1, keepdims=True) * (1.0 / D_MODEL) + EPS)
    acts = []
    for c in range(_SLABS):
        lanes = slice(c * LANES, (c + 1) * LANES)
        z = (conv_buf[c] - mu) * rstd * lng_ref[:, lanes] + lnb_ref[:, lanes]
        acts.append((z * _sigmoid(z)).astype(_BF16))
    out = _dot(jnp.concatenate(acts, axis=-1), wout_ref[...]) + bout_ref[...]
    o_ref[...] = x + _rmsnorm(out, gpost_ref[...])


def _conformer_conv(x, g_pre, w_in, b_in, w_dw, b_dw, ln_g, ln_b, w_out, b_out, g_post):
    batch, seq_len, _ = x.shape
    tiles = seq_len // ROW_TILE
    halo_per_tile = ROW_TILE // CONV_HALO
    last_halo = seq_len // CONV_HALO - 1
    return pl.pallas_call(
        _conv_kernel,
        grid=(batch, tiles),
        in_specs=[
            pl.BlockSpec((None, CONV_HALO, D_MODEL), lambda b, i: (b, jnp.maximum(i * halo_per_tile - 1, 0), 0)),
            pl.BlockSpec((None, ROW_TILE, D_MODEL), lambda b, i: (b, i, 0)),
            pl.BlockSpec((None, CONV_HALO, D_MODEL), lambda b, i: (b, jnp.minimum((i + 1) * halo_per_tile, last_halo), 0)),
            _resident((1, D_MODEL)),
            _resident((D_MODEL, 2 * D_MODEL)),
            _resident((1, 2 * D_MODEL)),
            _resident((CONV_WIDTH, D_MODEL)),
            _resident((1, D_MODEL)),
            _resident((1, D_MODEL)),
            _resident((1, D_MODEL)),
            _resident((D_MODEL, D_MODEL)),
            _resident((1, D_MODEL)),
            _resident((1, D_MODEL)),
        ],
        out_specs=pl.BlockSpec((None, ROW_TILE, D_MODEL), lambda b, i: (b, i, 0)),
        out_shape=jax.ShapeDtypeStruct((batch, seq_len, D_MODEL), _F32),
        scratch_shapes=[
            pltpu.VMEM((_SLABS, ROW_TILE + 2 * CONV_HALO, LANES), _F32),
            pltpu.VMEM((_SLABS, ROW_TILE, LANES), _F32),
        ],
        compiler_params=_params("parallel", "parallel"),
        name="conformer_conv",
    )(x, x, x, g_pre, w_in, b_in, w_dw, b_dw, ln_g, ln_b, w_out, b_out, g_post)


def _row(v):
    return v.reshape(1, -1)


def _trunk(x, p):
    batch, seq_len, _ = x.shape
    rows = batch * seq_len
    for layer in range(DEPTH):
        li = layer // N_MIXERS
        if layer % N_MIXERS == 0:
            qt, k, vt = _qkv_proj(x, _row(p["g_mix_pre"][layer]), p["w_qt"][li], p["w_k"][li], p["w_vt"][li])
            x = _attention(qt, k, vt, p["alibi"], p["sink_rows"][li], x, p["w_o"][li], _row(p["g_mix_post"][layer]))
        else:
            x = _conformer_conv(x, _row(p["g_mix_pre"][layer]), p["conv_w_in"][li], _row(p["conv_b_in"][li]),
                                p["conv_w_dw"][li], _row(p["conv_b_dw"][li]), _row(p["conv_ln_g"][li]),
                                _row(p["conv_ln_b"][li]), p["conv_w_out"][li], _row(p["conv_b_out"][li]),
                                _row(p["g_mix_post"][layer]))
        x2d = _ffn(x.reshape(rows, D_MODEL), _row(p["g_ffn_pre"][layer]), p["ffn_w_gu"][layer],
                   p["ffn_w_down"][layer], _row(p["g_ffn_post"][layer]))
        x = x2d.reshape(batch, seq_len, D_MODEL)
    return x


def kernel(x_prompt, x_sample, g_mix_pre, g_mix_post, g_ffn_pre, g_ffn_post, w_qkv, w_o, attn_sink,
           conv_w_in, conv_b_in, conv_w_dw, conv_b_dw, conv_ln_g, conv_ln_b, conv_w_out, conv_b_out,
           ffn_w_gu, ffn_w_down):
    n_layers = attn_sink.shape[0]
    sink_rows = jnp.repeat((attn_sink * LOG2E).reshape(n_layers, N_KV_HEADS, 1, GROUP), BLOCK, axis=-1)

    params = dict(
        g_mix_pre=g_mix_pre, g_mix_post=g_mix_post, g_ffn_pre=g_ffn_pre, g_ffn_post=g_ffn_post,
        w_qt=jnp.swapaxes(w_qkv[:, :, :Q_COLS], 1, 2).astype(_BF16),
        w_k=w_qkv[:, :, Q_COLS:Q_COLS + KV_HEAD_COLS].astype(_BF16),
        w_vt=jnp.swapaxes(w_qkv[:, :, Q_COLS + KV_HEAD_COLS:], 1, 2).astype(_BF16),
        w_o=w_o.astype(_BF16), alibi=jnp.asarray(_alibi_table()), sink_rows=sink_rows,
        conv_w_in=conv_w_in.astype(_BF16), conv_b_in=conv_b_in, conv_w_dw=conv_w_dw, conv_b_dw=conv_b_dw,
        conv_ln_g=conv_ln_g, conv_ln_b=conv_ln_b, conv_w_out=conv_w_out.astype(_BF16), conv_b_out=conv_b_out,
        ffn_w_gu=ffn_w_gu.astype(_BF16),
        ffn_w_down=ffn_w_down.astype(_BF16),
    )
    return (_trunk(x_prompt, params), _trunk(x_sample, params))
```
